```python
import math
import jax, jax.numpy as jnp
from jax import lax
import numpy as np

D_MODEL = 1024
BATCH = 2
SEQ = 8192
DEPTH = 1

ROPE_THETA = 500000.0
NORM_EPS = 1e-6
BLOCK_Q = 128
MLA_HEADS = 8
MLA_NOPE_DIM = 64
MLA_ROPE_DIM = 32
MLA_QK_DIM = MLA_NOPE_DIM + MLA_ROPE_DIM
MLA_V_DIM = 64
MLA_Q_RANK = 384
MLA_KV_RANK = 256
MLA_WIDTH = MLA_HEADS * MLA_V_DIM
DIFF_HEADS = 4
DIFF_HEAD_DIM = 64
DIFF_V_DIM = 2 * DIFF_HEAD_DIM
DIFF_ROT_DIM = DIFF_HEAD_DIM // 4
DIFF_QK_WIDTH = DIFF_HEADS * 2 * DIFF_HEAD_DIM
DIFF_WIDTH = DIFF_HEADS * DIFF_V_DIM
N_BRANCHES = 2
IN_SIZES = (MLA_Q_RANK, MLA_KV_RANK, MLA_ROPE_DIM, DIFF_QK_WIDTH, DIFF_QK_WIDTH, DIFF_WIDTH, N_BRANCHES * D_MODEL)
IN_WIDTH = MLA_Q_RANK + MLA_KV_RANK + MLA_ROPE_DIM + 2 * DIFF_QK_WIDTH + DIFF_WIDTH + N_BRANCHES * D_MODEL
FFN_HIDDEN = ((-(-8 * D_MODEL // 3) + 255) // 256) * 256

kernel_name = "hybrid_mla_diffattn_gated_block"


def rmsnorm(x, g):
    xf = x.astype(jnp.float32)
    y = xf * lax.rsqrt(jnp.mean(xf * xf, axis=-1, keepdims=True) + NORM_EPS)
    return (y * g.astype(jnp.float32)).astype(x.dtype)


def apply_rope(x, positions, rot_dim):
    half = rot_dim // 2
    inv_freq = jnp.exp(-math.log(ROPE_THETA) * jnp.arange(half, dtype=jnp.float32) * (2.0 / rot_dim))
    ang = positions.astype(jnp.float32)[..., None] * inv_freq
    ang = ang.reshape(ang.shape[:2] + (1,) * (x.ndim - 3) + (half,))
    cos, sin = jnp.cos(ang), jnp.sin(ang)
    xr = x[..., :rot_dim].astype(jnp.float32)
    x1, x2 = xr[..., :half], xr[..., half:]
    rot = jnp.concatenate([x1 * cos - x2 * sin, x2 * cos + x1 * sin], axis=-1).astype(x.dtype)
    return jnp.concatenate([rot, x[..., rot_dim:]], axis=-1)


def causal_attention(q, k, v):
    B, S, H, Dk = q.shape
    Dv = v.shape[-1]
    nb = S // BLOCK_Q
    scale = Dk ** -0.5
    qb = q.reshape(B, nb, BLOCK_Q, H, Dk).transpose(1, 0, 2, 3, 4)
    kpos = jnp.arange(S)

    def one_block(args):
        qi, i = args
        s = jnp.einsum('bqhd,bkhd->bhqk', qi, k).astype(jnp.float32) * scale
        qpos = i * BLOCK_Q + jnp.arange(BLOCK_Q)
        s = jnp.where(kpos[None, :] <= qpos[:, None], s, -jnp.inf)
        p = jax.nn.softmax(s, axis=-1).astype(v.dtype)
        return jnp.einsum('bhqk,bkhd->bqhd', p, v)

    out = lax.map(one_block, (qb, jnp.arange(nb)))
    return out.transpose(1, 0, 2, 3, 4).reshape(B, S, H, Dv)


def setup_inputs(seed: int = 0) -> dict:
    key = jax.random.key(seed)
    ks = jax.random.split(key, 24)

    def dense(k, fan_in, fan_out):
        return jax.random.normal(k, (DEPTH, fan_in, fan_out), jnp.float32) * fan_in ** -0.5

    def gain(k, n):
        return 1.0 + 0.02 * jax.random.normal(k, (DEPTH, n), jnp.float32)

    def small(k, n, s):
        return s * jax.random.normal(k, (DEPTH, n), jnp.float32)

    x = jax.random.normal(ks[0], (BATCH, SEQ, D_MODEL), jnp.float32)
    positions = jnp.broadcast_to(jnp.arange(SEQ, dtype=jnp.int32)[None, :], (BATCH, SEQ))
    return {
        "x": x,
        "positions": positions,
        "norm_mix_g": gain(ks[1], D_MODEL),
        "w_in": dense(ks[2], D_MODEL, IN_WIDTH),
        "b_gate": small(ks[3], N_BRANCHES * D_MODEL, 0.1),
        "mla_q_norm_g": gain(ks[4], MLA_Q_RANK),
        "mla_w_uq": dense(ks[5], MLA_Q_RANK, MLA_HEADS * MLA_QK_DIM),
        "mla_kv_norm_g": gain(ks[6], MLA_KV_RANK),
        "mla_w_ukv": dense(ks[7], MLA_KV_RANK, MLA_HEADS * (MLA_NOPE_DIM + MLA_V_DIM)),
        "diff_lambda_q1": small(ks[8], DIFF_HEAD_DIM, 0.1),
        "diff_lambda_k1": small(ks[9], DIFF_HEAD_DIM, 0.1),
        "diff_lambda_q2": small(ks[10], DIFF_HEAD_DIM, 0.1),
        "diff_lambda_k2": small(ks[11], DIFF_HEAD_DIM, 0.1),
        "diff_subln_g": gain(ks[12], DIFF_V_DIM),
        "w_branch_mla": dense(ks[13], MLA_WIDTH, D_MODEL),
        "w_branch_diff": dense(ks[14], DIFF_WIDTH, D_MODEL),
        "w_out": dense(ks[15], D_MODEL, D_MODEL),
        "norm_ffn_g": gain(ks[16], D_MODEL),
        "w_ffn_gate": dense(ks[17], D_MODEL, FFN_HIDDEN),
        "w_ffn_up": dense(ks[18], D_MODEL, FFN_HIDDEN),
        "w_ffn_down": dense(ks[19], FFN_HIDDEN, D_MODEL),
        "norm_final_g": 1.0 + 0.02 * jax.random.normal(ks[20], (D_MODEL,), jnp.float32),
    }


def reference(x, positions, norm_mix_g, w_in, b_gate, mla_q_norm_g, mla_w_uq, mla_kv_norm_g, mla_w_ukv,
              diff_lambda_q1, diff_lambda_k1, diff_lambda_q2, diff_lambda_k2, diff_subln_g,
              w_branch_mla, w_branch_diff, w_out, norm_ffn_g, w_ffn_gate, w_ffn_up, w_ffn_down, norm_final_g):
    B, S, _ = x.shape
    split_idx = [int(v) for v in np.cumsum(IN_SIZES)[:-1]]
    for l in range(DEPTH):
        xn = rmsnorm(x, norm_mix_g[l])
        proj = jnp.einsum('bsd,de->bse', xn, w_in[l])
        q_lat, kv_lat, k_rope, dq, dk, dv, gate_pre = jnp.split(proj, split_idx, axis=-1)

        q = jnp.einsum('bsr,re->bse', rmsnorm(q_lat, mla_q_norm_g[l]), mla_w_uq[l])
        q = q.reshape(B, S, MLA_HEADS, MLA_QK_DIM)
        q_nope, q_pe = q[..., :MLA_NOPE_DIM], apply_rope(q[..., MLA_NOPE_DIM:], positions, MLA_ROPE_DIM)
        kv = jnp.einsum('bsr,re->bse', rmsnorm(kv_lat, mla_kv_norm_g[l]), mla_w_ukv[l])
        kv = kv.reshape(B, S, MLA_HEADS, MLA_NOPE_DIM + MLA_V_DIM)
        k_nope, v_mla = kv[..., :MLA_NOPE_DIM], kv[..., MLA_NOPE_DIM:]
        k_pe = apply_rope(k_rope[:, :, None, :], positions, MLA_ROPE_DIM)
        q_mla = jnp.concatenate([q_nope, q_pe], axis=-1)
        k_mla = jnp.concatenate([k_nope, jnp.broadcast_to(k_pe, (B, S, MLA_HEADS, MLA_ROPE_DIM))], axis=-1)
        o_mla = causal_attention(q_mla, k_mla, v_mla).reshape(B, S, MLA_WIDTH)

        dq = apply_rope(dq.reshape(B, S, DIFF_HEADS, 2, DIFF_HEAD_DIM), positions, DIFF_ROT_DIM)
        dk = apply_rope(dk.reshape(B, S, DIFF_HEADS, 2, DIFF_HEAD_DIM), positions, DIFF_ROT_DIM)
        dv = dv.reshape(B, S, DIFF_HEADS, DIFF_V_DIM)
        o1 = causal_attention(dq[..., 0, :], dk[..., 0, :], dv)
        o2 = causal_attention(dq[..., 1, :], dk[..., 1, :], dv)
        lambda_init = 0.8 - 0.6 * math.exp(-0.3 * l)
        lam = (jnp.exp(jnp.sum(diff_lambda_q1[l].astype(jnp.float32) * diff_lambda_k1[l].astype(jnp.float32)))
               - jnp.exp(jnp.sum(diff_lambda_q2[l].astype(jnp.float32) * diff_lambda_k2[l].astype(jnp.float32)))
               + lambda_init).astype(o1.dtype)
        o_diff = rmsnorm(o1 - lam * o2, diff_subln_g[l]) * (1.0 - lambda_init)
        o_diff = o_diff.reshape(B, S, DIFF_WIDTH)

        gates = jax.nn.sigmoid(gate_pre + b_gate[l])
        g_mla, g_diff = gates[..., :D_MODEL], gates[..., D_MODEL:]
        merged = (g_mla * jnp.einsum('bse,ed->bsd', o_mla, w_branch_mla[l])
                  + g_diff * jnp.einsum('bse,ed->bsd', o_diff, w_branch_diff[l]))
        x = x + jnp.einsum('bsd,de->bse', merged, w_out[l])

        hn = rmsnorm(x, norm_ffn_g[l])
        hid = jax.nn.silu(jnp.einsum('bsd,df->bsf', hn, w_ffn_gate[l])) * jnp.einsum('bsd,df->bsf', hn, w_ffn_up[l])
        x = x + jnp.einsum('bsf,fd->bsd', hid, w_ffn_down[l])
    return rmsnorm(x, norm_final_g)
```

```python
import functools
import math

import jax
import jax.numpy as jnp
from jax import lax
from jax.experimental import pallas as pl
from jax.experimental.pallas import tpu as pltpu

F32 = jnp.float32
BF16 = jnp.bfloat16

ROPE_THETA = 500000.0
NORM_EPS = 1e-6
LANES = 128

MLA_HEADS = 8
MLA_NOPE_DIM = 64
MLA_ROPE_DIM = 32
MLA_QK_DIM = MLA_NOPE_DIM + MLA_ROPE_DIM
MLA_V_DIM = 64
MLA_Q_RANK = 384
MLA_KV_RANK = 256
DIFF_HEADS = 4
DIFF_HEAD_DIM = 64
DIFF_V_DIM = 2 * DIFF_HEAD_DIM
DIFF_ROT_DIM = DIFF_HEAD_DIM // 4

VMEM_LIMIT_BYTES = 48 * 1024 * 1024

_C_QLAT = 0
_C_KVLAT = _C_QLAT + MLA_Q_RANK
_C_KPE = _C_KVLAT + MLA_KV_RANK
_C_DQ = _C_KPE + LANES
_C_DK = _C_DQ + DIFF_HEADS * LANES
_C_DV = _C_DK + DIFF_HEADS * LANES
_C_GATE = _C_DV + DIFF_HEADS * DIFF_V_DIM


def _rms(x, g):
    return x * lax.rsqrt(jnp.mean(x * x, axis=-1, keepdims=True) + NORM_EPS) * g


def _dot(a, b):
    return jnp.dot(a, b, preferred_element_type=F32)


def _dot_nt(a, b):
    return lax.dot_general(a, b, (((1,), (1,)), ((), ())), preferred_element_type=F32)


def _rope_slab(x, cos, sin_lo, sin_hi, half):
    return x * cos + pltpu.roll(x, LANES - half, 1) * sin_lo + pltpu.roll(x, half, 1) * sin_hi


def _proj_kernel(x_ref, pos_ref, rc_ref, gmix_ref, w1_ref, gq_ref, wuq_ref, gkv_ref, wuk_ref, wuv_ref, bg_ref,
                 q_out, k_out, v_out, dq_out, dk_out, dv_out, gate_out, *, mla_scale, diff_scale):
    x = x_ref[...]
    xn = _rms(x, gmix_ref[...]).astype(BF16)
    pos = pos_ref[...]

    ang = pos * rc_ref[0:1, :]
    cos_m, sin_m = jnp.cos(ang), jnp.sin(ang)
    sin_m_lo, sin_m_hi = sin_m * rc_ref[1:2, :], sin_m * rc_ref[2:3, :]
    ang = pos * rc_ref[3:4, :]
    cos_d, sin_d = jnp.cos(ang), jnp.sin(ang)
    sin_d_lo, sin_d_hi = sin_d * rc_ref[4:5, :], sin_d * rc_ref[5:6, :]

    q_lat = _dot(xn, w1_ref[:, _C_QLAT:_C_QLAT + MLA_Q_RANK])
    q = _dot(_rms(q_lat, gq_ref[...]).astype(BF16), wuq_ref[...])
    for h in range(MLA_HEADS):
        sl = slice(h * LANES, (h + 1) * LANES)
        q_out[:, sl] = (_rope_slab(q[:, sl], cos_m, sin_m_lo, sin_m_hi, MLA_ROPE_DIM // 2) * mla_scale).astype(BF16)

    kv_lat = _dot(xn, w1_ref[:, _C_KVLAT:_C_KVLAT + MLA_KV_RANK])
    kvn = _rms(kv_lat, gkv_ref[...]).astype(BF16)
    k_nope = _dot(kvn, wuk_ref[...])
    k_pe = _rope_slab(_dot(xn, w1_ref[:, _C_KPE:_C_KPE + LANES]), cos_m, sin_m_lo, sin_m_hi, MLA_ROPE_DIM // 2)
    for h in range(MLA_HEADS):
        sl = slice(h * LANES, (h + 1) * LANES)
        k_out[:, sl] = (k_nope[:, sl] + k_pe).astype(BF16)
    v_out[...] = _dot(kvn, wuv_ref[...]).astype(BF16)

    dq = _dot(xn, w1_ref[:, _C_DQ:_C_DQ + DIFF_HEADS * LANES])
    dk = _dot(xn, w1_ref[:, _C_DK:_C_DK + DIFF_HEADS * LANES])
    for h in range(DIFF_HEADS):
        sl = slice(h * LANES, (h + 1) * LANES)
        dq_out[:, sl] = (_rope_slab(dq[:, sl], cos_d, sin_d_lo, sin_d_hi, DIFF_ROT_DIM // 2) * diff_scale).astype(BF16)
        dk_out[:, sl] = _rope_slab(dk[:, sl], cos_d, sin_d_lo, sin_d_hi, DIFF_ROT_DIM // 2).astype(BF16)
    dv_out[...] = _dot(xn, w1_ref[:, _C_DV:_C_DV + DIFF_HEADS * DIFF_V_DIM]).astype(BF16)

    gate_pre = _dot(xn, w1_ref[:, _C_GATE:]) + bg_ref[...]
    gate_out[...] = (1.0 / (1.0 + jnp.exp(-gate_pre))).astype(BF16)


def _const_spec(shape):
    return pl.BlockSpec(shape, lambda *_: (0,) * len(shape), pipeline_mode=pl.Buffered(1))


def _proj_call(x2, pos2, rc, gmix, w1, gq, wuq, gkv, wuk, wuv, bg, *, tm):
    T, D = x2.shape
    n_gate = bg.shape[1]
    row = lambda w: pl.BlockSpec((tm, w), lambda i: (i, 0))
    outs = [(MLA_HEADS * LANES), (MLA_HEADS * LANES), MLA_HEADS * MLA_V_DIM,
            DIFF_HEADS * LANES, DIFF_HEADS * LANES, DIFF_HEADS * DIFF_V_DIM, n_gate]
    return pl.pallas_call(
        functools.partial(_proj_kernel, mla_scale=MLA_QK_DIM ** -0.5, diff_scale=DIFF_HEAD_DIM ** -0.5),
        grid=(T // tm,),
        in_specs=[row(D), row(1)] + [_const_spec(a.shape) for a in (rc, gmix, w1, gq, wuq, gkv, wuk, wuv, bg)],
        out_specs=[row(w) for w in outs],
        out_shape=[jax.ShapeDtypeStruct((T, w), BF16) for w in outs],
        compiler_params=pltpu.CompilerParams(dimension_semantics=("arbitrary",), vmem_limit_bytes=VMEM_LIMIT_BYTES),
        name="proj",
    )(x2, pos2, rc, gmix, w1, gq, wuq, gkv, wuk, wuv, bg)


def _flash_step(q, kb, vb, m, l, acc, mask):
    s = _dot_nt(q, kb)
    if mask is not None:
        s = jnp.where(mask, s, -jnp.inf)
    m_new = jnp.maximum(m, jnp.max(s, axis=-1, keepdims=True))
    p = jnp.exp(s - m_new)
    alpha = jnp.exp(m - m_new)
    l = alpha * l + jnp.sum(p, axis=-1, keepdims=True)
    acc = alpha * acc + _dot(p.astype(BF16), vb)
    return m_new, l, acc


def _flash_init(tq):
    return (jnp.full((tq, 1), -jnp.inf, F32), jnp.zeros((tq, 1), F32), jnp.zeros((tq, LANES), F32))


def _diag_mask(t):
    return lax.broadcasted_iota(jnp.int32, (t, t), 1) <= lax.broadcasted_iota(jnp.int32, (t, t), 0)


def _mla_attn_kernel(q_ref, k_ref, v_ref, o_ref, *, t):
    i = pl.program_id(2)
    mask = _diag_mask(t)
    res = []
    for hh in range(2):
        sl = slice(hh * LANES, (hh + 1) * LANES)
        q = q_ref[:, sl]

        def body(j, carry, q=q, sl=sl):
            rows = pl.ds(pl.multiple_of(j * t, t), t)
            return _flash_step(q, k_ref[rows, sl], v_ref[rows, :], *carry, None)

        carry = lax.fori_loop(0, i, body, _flash_init(t))
        rows = pl.ds(pl.multiple_of(i * t, t), t)
        m, l, acc = _flash_step(q, k_ref[rows, sl], v_ref[rows, :], *carry, mask)
        res.append(acc / l)
    lane = lax.broadcasted_iota(jnp.int32, (t, LANES), 1)
    o_ref[...] = jnp.where(lane < MLA_V_DIM, res[0], res[1]).astype(o_ref.dtype)


def _mla_attn_call(q, k, v, *, t):
    B, S, _ = q.shape
    return pl.pallas_call(
        functools.partial(_mla_attn_kernel, t=t),
        grid=(B, MLA_HEADS // 2, S // t),
        in_specs=[pl.BlockSpec((None, t, 2 * LANES), lambda b, h, i: (b, i, h)),
                  pl.BlockSpec((None, S, 2 * LANES), lambda b, h, i: (b, 0, h)),
                  pl.BlockSpec((None, S, LANES), lambda b, h, i: (b, 0, h))],
        out_specs=pl.BlockSpec((None, t, LANES), lambda b, h, i: (b, i, h)),
        out_shape=jax.ShapeDtypeStruct((B, S, MLA_HEADS * MLA_V_DIM), BF16),
        compiler_params=pltpu.CompilerParams(dimension_semantics=("arbitrary",) * 3,
                                             vmem_limit_bytes=VMEM_LIMIT_BYTES),
        name="mla_attn",
    )(q, k, v)


def _diff_attn_kernel(lq1_ref, lk1_ref, lq2_ref, lk2_ref, gsub_ref, q_ref, k_ref, v_ref, o_ref, *, t, lambda_init):
    i = pl.program_id(2)
    mask = _diag_mask(t)
    q = q_ref[...]
    lane = lax.broadcasted_iota(jnp.int32, (t, LANES), 1)
    zero = jnp.zeros_like(q)
    q1 = jnp.where(lane < DIFF_HEAD_DIM, q, zero)
    q2 = jnp.where(lane >= DIFF_HEAD_DIM, q, zero)

    def step(j, carry, msk):
        rows = pl.ds(pl.multiple_of(j * t, t), t)
        kb, vb = k_ref[rows, :], v_ref[rows, :]
        return _flash_step(q1, kb, vb, *carry[:3], msk) + _flash_step(q2, kb, vb, *carry[3:], msk)

    carry = lax.fori_loop(0, i, lambda j, c: step(j, c, None), _flash_init(t) + _flash_init(t))
    m1, l1, a1, m2, l2, a2 = step(i, carry, mask)

    lam = (jnp.exp(jnp.sum(lq1_ref[...] * lk1_ref[...], axis=-1, keepdims=True))
           - jnp.exp(jnp.sum(lq2_ref[...] * lk2_ref[...], axis=-1, keepdims=True)) + lambda_init)
    d = a1 / l1 - lam * (a2 / l2)
    o_ref[...] = (_rms(d, gsub_ref[...]) * (1.0 - lambda_init)).astype(o_ref.dtype)


def _diff_attn_call(lq1, lk1, lq2, lk2, gsub, q, k, v, *, t, lambda_init):
    B, S, _ = q.shape
    tile = pl.BlockSpec((None, t, LANES), lambda b, h, i: (b, i, h))
    full = pl.BlockSpec((None, S, LANES), lambda b, h, i: (b, 0, h))
    return pl.pallas_call(
        functools.partial(_diff_attn_kernel, t=t, lambda_init=lambda_init),
        grid=(B, DIFF_HEADS, S // t),
        in_specs=[_const_spec(a.shape) for a in (lq1, lk1, lq2, lk2, gsub)] + [tile, full, full],
        out_specs=tile,
        out_shape=jax.ShapeDtypeStruct((B, S, DIFF_HEADS * DIFF_V_DIM), BF16),
        compiler_params=pltpu.CompilerParams(dimension_semantics=("arbitrary",) * 3,
                                             vmem_limit_bytes=VMEM_LIMIT_BYTES),
        name="diff_attn",
    )(lq1, lk1, lq2, lk2, gsub, q, k, v)


def _post_kernel(x_ref, om_ref, od_ref, gate_ref, wbm_ref, wbd_ref, wo_ref, gffn_ref, wg_ref, wu_ref, wd_ref,
                 gfin_ref, out_ref):
    d_model = x_ref.shape[1]
    gates = gate_ref[...].astype(F32)
    merged = (gates[:, :d_model] * _dot(om_ref[...], wbm_ref[...])
              + gates[:, d_model:] * _dot(od_ref[...], wbd_ref[...]))
    x = x_ref[...] + _dot(merged.astype(BF16), wo_ref[...])
    hn = _rms(x, gffn_ref[...]).astype(BF16)
    a = _dot(hn, wg_ref[...])
    hid = (a * (1.0 / (1.0 + jnp.exp(-a)))) * _dot(hn, wu_ref[...])
    x = x + _dot(hid.astype(BF16), wd_ref[...])
    out_ref[...] = _rms(x, gfin_ref[...])


def _post_call(x2, om, od, gates, wbm, wbd, wo, gffn, wg, wu, wd, gfin, *, tm):
    T, D = x2.shape
    row = lambda a: pl.BlockSpec((tm, a.shape[1]), lambda i: (i, 0))
    consts = (wbm, wbd, wo, gffn, wg, wu, wd, gfin)
    return pl.pallas_call(
        _post_kernel,
        grid=(T // tm,),
        in_specs=[row(a) for a in (x2, om, od, gates)] + [_const_spec(a.shape) for a in consts],
        out_specs=pl.BlockSpec((tm, D), lambda i: (i, 0)),
        out_shape=jax.ShapeDtypeStruct((T, D), F32),
        compiler_params=pltpu.CompilerParams(dimension_semantics=("arbitrary",), vmem_limit_bytes=VMEM_LIMIT_BYTES),
        name="post",
    )(x2, om, od, gates, *consts)


def _rope_consts():
    lane = jnp.arange(LANES)

    def rows(r, rot_dim):
        half = rot_dim // 2
        inv_freq = jnp.exp(-math.log(ROPE_THETA) * jnp.arange(half, dtype=F32) * (2.0 / rot_dim))
        in_rot = (r >= 0) & (r < rot_dim)
        freq = jnp.where(in_rot, inv_freq[jnp.clip(r, 0, rot_dim - 1) % half], 0.0)
        lo = jnp.where((r >= 0) & (r < half), -1.0, 0.0)
        hi = jnp.where((r >= half) & (r < rot_dim), 1.0, 0.0)
        return [freq, lo, hi]

    z = jnp.zeros((LANES,), F32)
    return jnp.stack(rows(lane - MLA_NOPE_DIM, MLA_ROPE_DIM) + rows(lane % DIFF_HEAD_DIM, DIFF_ROT_DIM)
                     + [z, z]).astype(F32)


def _pad_last(a, width):
    return jnp.pad(a, [(0, 0)] * (a.ndim - 1) + [(0, width - a.shape[-1])])


def kernel(x, positions, norm_mix_g, w_in, b_gate, mla_q_norm_g, mla_w_uq, mla_kv_norm_g, mla_w_ukv, diff_lambda_q1, diff_lambda_k1, diff_lambda_q2, diff_lambda_k2, diff_subln_g, w_branch_mla, w_branch_diff, w_out, norm_ffn_g, w_ffn_gate, w_ffn_up, w_ffn_down, norm_final_g):
    B, S, D = x.shape
    T = B * S
    assert w_in.shape[0] == 1, "single-layer block"
    layer = 0
    lambda_init = 0.8 - 0.6 * math.exp(-0.3 * layer)
    row = lambda a: a.reshape(1, -1)

    w = w_in[layer]
    o_q, o_kv, o_kr = MLA_Q_RANK, MLA_Q_RANK + MLA_KV_RANK, MLA_Q_RANK + MLA_KV_RANK + MLA_ROPE_DIM
    k_rope_w = jnp.pad(w[:, o_kv:o_kr], ((0, 0), (MLA_NOPE_DIM, LANES - MLA_QK_DIM)))
    w1 = jnp.concatenate([w[:, :o_kv], k_rope_w, w[:, o_kr:]], axis=1).astype(BF16)
    wuq = _pad_last(mla_w_uq[layer].reshape(MLA_Q_RANK, MLA_HEADS, MLA_QK_DIM), LANES)
    wuq = wuq.reshape(MLA_Q_RANK, MLA_HEADS * LANES).astype(BF16)
    wukv = mla_w_ukv[layer].reshape(MLA_KV_RANK, MLA_HEADS, MLA_NOPE_DIM + MLA_V_DIM)
    wuk = _pad_last(wukv[..., :MLA_NOPE_DIM], LANES).reshape(MLA_KV_RANK, MLA_HEADS * LANES).astype(BF16)
    wuv = wukv[..., MLA_NOPE_DIM:].reshape(MLA_KV_RANK, MLA_HEADS * MLA_V_DIM).astype(BF16)

    x2 = x.reshape(T, D)
    pos2 = positions.astype(F32).reshape(T, 1)
    q, k, v, dq, dk, dv, gates = _proj_call(
        x2, pos2, _rope_consts(), row(norm_mix_g[layer]), w1, row(mla_q_norm_g[layer]), wuq,
        row(mla_kv_norm_g[layer]), wuk, wuv, row(b_gate[layer]), tm=256)

    shp = lambda a: a.reshape(B, S, a.shape[-1])
    o_mla = _mla_attn_call(shp(q), shp(k), shp(v), t=512)
    o_diff = _diff_attn_call(row(diff_lambda_q1[layer]), row(diff_lambda_k1[layer]), row(diff_lambda_q2[layer]),
                             row(diff_lambda_k2[layer]), row(diff_subln_g[layer]), shp(dq), shp(dk), shp(dv),
                             t=512, lambda_init=lambda_init)

    out = _post_call(x2, o_mla.reshape(T, -1), o_diff.reshape(T, -1), gates,
                     w_branch_mla[layer].astype(BF16), w_branch_diff[layer].astype(BF16), w_out[layer].astype(BF16),
                     row(norm_ffn_g[layer]), w_ffn_gate[layer].astype(BF16), w_ffn_up[layer].astype(BF16),
                     w_ffn_down[layer].astype(BF16), row(norm_final_g), tm=256)
    return out.reshape(B, S, D)
```

```python
import functools
import math

import jax
import jax.numpy as jnp
from jax import lax
from jax.experimental import pallas as pl
from jax.experimental.pallas import tpu as pltpu

F32 = jnp.float32
BF16 = jnp.bfloat16

ROPE_THETA = 500000.0
NORM_EPS = 1e-6
LANES = 128
LOG2E = math.log2(math.e)

MLA_HEADS = 8
MLA_NOPE_DIM = 64
MLA_ROPE_DIM = 32
MLA_QK_DIM = MLA_NOPE_DIM + MLA_ROPE_DIM
MLA_V_DIM = 64
MLA_Q_RANK = 384
MLA_KV_RANK = 256
DIFF_HEADS = 4
DIFF_HEAD_DIM = 64
DIFF_V_DIM = 2 * DIFF_HEAD_DIM
DIFF_ROT_DIM = DIFF_HEAD_DIM // 4

VMEM_LIMIT_BYTES = 48 * 1024 * 1024
TOKEN_TILE = 256
QUERY_TILE = 2 * TOKEN_TILE

_C_QLAT = 0
_C_KVLAT = _C_QLAT + MLA_Q_RANK
_C_KPE = _C_KVLAT + MLA_KV_RANK
_C_DQ = _C_KPE + LANES
_C_DK = _C_DQ + DIFF_HEADS * LANES
_C_GATE = _C_DK + DIFF_HEADS * LANES


def _rms(x, g):
    return x * lax.rsqrt(jnp.mean(x * x, axis=-1, keepdims=True) + NORM_EPS) * g


def _dot(a, b):
    return jnp.dot(a, b, preferred_element_type=F32)


def _dot_nt(a, b):
    return lax.dot_general(a, b, (((1,), (1,)), ((), ())), preferred_element_type=F32)


def _rope_slab(x, cos, sin_lo, sin_hi, half):
    return x * cos + pltpu.roll(x, LANES - half, 1) * sin_lo + pltpu.roll(x, half, 1) * sin_hi


def _const_spec(shape):
    return pl.BlockSpec(shape, lambda *_: (0,) * len(shape), pipeline_mode=pl.Buffered(1))


def _params(n_grid_dims):
    return pltpu.CompilerParams(dimension_semantics=("arbitrary",) * n_grid_dims, vmem_limit_bytes=VMEM_LIMIT_BYTES)


def _proj_kernel(x_ref, pos_ref, rc_ref, gmix_ref, w1_ref, gq_ref, wuq_ref, gkv_ref, wuk_ref, wuvt_ref, wdvt_ref,
                 bg_ref, q_out, k_out, vt_out, dq_out, dk_out, dvt_out, gate_out, *, mla_scale, diff_scale):
    x = x_ref[...]
    xn = _rms(x, gmix_ref[...]).astype(BF16)
    pos = pos_ref[...]

    ang = pos * rc_ref[0:1, :]
    cos_m, sin_m = jnp.cos(ang), jnp.sin(ang)
    sin_m_lo, sin_m_hi = sin_m * rc_ref[1:2, :], sin_m * rc_ref[2:3, :]
    ang = pos * rc_ref[3:4, :]
    cos_d, sin_d = jnp.cos(ang), jnp.sin(ang)
    sin_d_lo, sin_d_hi = sin_d * rc_ref[4:5, :], sin_d * rc_ref[5:6, :]

    q_lat = _dot(xn, w1_ref[:, _C_QLAT:_C_QLAT + MLA_Q_RANK])
    q = _dot(_rms(q_lat, gq_ref[...]).astype(BF16), wuq_ref[...])
    for h in range(MLA_HEADS):
        sl = slice(h * LANES, (h + 1) * LANES)
        q_out[:, sl] = (_rope_slab(q[:, sl], cos_m, sin_m_lo, sin_m_hi, MLA_ROPE_DIM // 2) * mla_scale).astype(BF16)

    kv_lat = _dot(xn, w1_ref[:, _C_KVLAT:_C_KVLAT + MLA_KV_RANK])
    kvn = _rms(kv_lat, gkv_ref[...]).astype(BF16)
    k_nope = _dot(kvn, wuk_ref[...])
    k_pe = _rope_slab(_dot(xn, w1_ref[:, _C_KPE:_C_KPE + LANES]), cos_m, sin_m_lo, sin_m_hi, MLA_ROPE_DIM // 2)
    for h in range(MLA_HEADS):
        sl = slice(h * LANES, (h + 1) * LANES)
        k_out[:, sl] = (k_nope[:, sl] + k_pe).astype(BF16)
    vt_out[...] = _dot_nt(wuvt_ref[...], kvn).astype(BF16)

    dq = _dot(xn, w1_ref[:, _C_DQ:_C_DQ + DIFF_HEADS * LANES])
    dk = _dot(xn, w1_ref[:, _C_DK:_C_DK + DIFF_HEADS * LANES])
    for h in range(DIFF_HEADS):
        sl = slice(h * LANES, (h + 1) * LANES)
        dq_out[:, sl] = (_rope_slab(dq[:, sl], cos_d, sin_d_lo, sin_d_hi, DIFF_ROT_DIM // 2) * diff_scale).astype(BF16)
        dk_out[:, sl] = _rope_slab(dk[:, sl], cos_d, sin_d_lo, sin_d_hi, DIFF_ROT_DIM // 2).astype(BF16)
    dvt_out[...] = _dot_nt(wdvt_ref[...], xn).astype(BF16)

    gate_pre = _dot(xn, w1_ref[:, _C_GATE:]) + bg_ref[...]
    gate_out[...] = (1.0 / (1.0 + jnp.exp(-gate_pre))).astype(BF16)


def _proj_call(x, pos, rc, gmix, w1, gq, wuq, gkv, wuk, wuvt, wdvt, bg):
    B, S, D = x.shape
    tm = TOKEN_TILE
    row = lambda w: pl.BlockSpec((None, tm, w), lambda b, i: (b, i, 0))
    col = lambda w: pl.BlockSpec((None, None, w, tm), lambda b, i: (b, i, 0, 0))
    consts = (rc, gmix, w1, gq, wuq, gkv, wuk, wuvt, wdvt, bg)
    row_w = lambda w: jax.ShapeDtypeStruct((B, S, w), BF16)
    col_w = lambda w: jax.ShapeDtypeStruct((B, S // tm, w, tm), BF16)
    wq, wv, wd = MLA_HEADS * LANES, MLA_HEADS * MLA_V_DIM, DIFF_HEADS * LANES
    return pl.pallas_call(
        functools.partial(_proj_kernel, mla_scale=MLA_QK_DIM ** -0.5 * LOG2E, diff_scale=DIFF_HEAD_DIM ** -0.5 * LOG2E),
        grid=(B, S // tm),
        in_specs=[row(D), row(1)] + [_const_spec(a.shape) for a in consts],
        out_specs=[row(wq), row(wq), col(wv), row(wd), row(wd), col(wd), row(bg.shape[1])],
        out_shape=[row_w(wq), row_w(wq), col_w(wv), row_w(wd), row_w(wd), col_w(wd), row_w(bg.shape[1])],
        compiler_params=_params(2),
        name="proj",
    )(x, pos, *consts)


def _flash_head(q, k_blk, vt_blk, sa_ref, sb_ref, acc_ref, i):
    tk, tq = sa_ref.shape
    key = lax.broadcasted_iota(jnp.int32, (tk, tq), 0)
    qry = lax.broadcasted_iota(jnp.int32, (tk, tq), 1)

    def scores(j):
        return _dot_nt(k_blk(j), q)

    def absorb(s_ref, j, m, l, mask):
        load = (lambda: s_ref[...]) if mask is None else (lambda: jnp.where(mask, s_ref[...], -jnp.inf))
        m_new = jnp.maximum(m, jnp.max(load(), axis=0, keepdims=True))
        p = jnp.exp2(load() - m_new)
        alpha = jnp.exp2(m - m_new)
        l = alpha * l + jnp.sum(p, axis=0, keepdims=True)
        acc_ref[...] = alpha * acc_ref[...] + _dot(vt_blk(j), p.astype(BF16))
        return m_new, l

    acc_ref[...] = jnp.zeros(acc_ref.shape, F32)
    sa_ref[...] = scores(0)

    def pair(pp, carry):
        m, l = carry
        sb_ref[...] = scores(2 * pp + 1)
        m, l = absorb(sa_ref, 2 * pp, m, l, None)
        sa_ref[...] = scores(2 * pp + 2)
        return absorb(sb_ref, 2 * pp + 1, m, l, None)

    init = (jnp.full((1, tq), -jnp.inf, F32), jnp.zeros((1, tq), F32))
    m, l = lax.fori_loop(0, i, pair, init)
    sb_ref[...] = scores(2 * i + 1)
    m, l = absorb(sa_ref, 2 * i, m, l, key <= qry)
    m, l = absorb(sb_ref, 2 * i + 1, m, l, key + tk <= qry)
    return l


def _key_rows(j, tk):
    return pl.ds(pl.multiple_of(j * tk, tk), tk)


def _mla_attn_kernel(q_ref, k_ref, vt_ref, o_ref, sa_ref, sb_ref, acc_ref):
    i = pl.program_id(2)
    tk = sa_ref.shape[0]
    outs = []
    for hh in range(2):
        sl = slice(hh * LANES, (hh + 1) * LANES)
        rows = slice(hh * MLA_V_DIM, (hh + 1) * MLA_V_DIM)
        acc = acc_ref.at[rows]
        l = _flash_head(q_ref[:, sl], lambda j, sl=sl: k_ref[_key_rows(j, tk), sl],
                        lambda j, rows=rows: vt_ref[j, rows, :], sa_ref, sb_ref, acc, i)
        outs.append(acc[...] / l)
    o_ref[...] = jnp.concatenate(outs, axis=0).T.astype(o_ref.dtype)


def _attn_scratch(tq, tk, n_acc):
    return [pltpu.VMEM((tk, tq), F32), pltpu.VMEM((tk, tq), F32)] + [pltpu.VMEM((LANES, tq), F32)] * n_acc


def _mla_attn_call(q, k, vt):
    B, S, _ = q.shape
    tq, tk = QUERY_TILE, TOKEN_TILE
    return pl.pallas_call(
        _mla_attn_kernel,
        grid=(B, MLA_HEADS // 2, S // tq),
        in_specs=[pl.BlockSpec((None, tq, 2 * LANES), lambda b, h, i: (b, i, h)),
                  pl.BlockSpec((None, S, 2 * LANES), lambda b, h, i: (b, 0, h)),
                  pl.BlockSpec((None, S // tk, LANES, tk), lambda b, h, i: (b, 0, h, 0))],
        out_specs=pl.BlockSpec((None, tq, LANES), lambda b, h, i: (b, i, h)),
        out_shape=jax.ShapeDtypeStruct((B, S, MLA_HEADS * MLA_V_DIM), BF16),
        scratch_shapes=_attn_scratch(tq, tk, 1),
        compiler_params=_params(3),
        name="mla_attn",
    )(q, k, vt)


def _diff_attn_kernel(lq1_ref, lk1_ref, lq2_ref, lk2_ref, gsub_ref, q_ref, k_ref, vt_ref, o_ref,
                      sa_ref, sb_ref, acc1_ref, acc2_ref, *, lambda_init):
    i = pl.program_id(2)
    tk = sa_ref.shape[0]
    q = q_ref[...]
    lane = lax.broadcasted_iota(jnp.int32, q.shape, 1)
    zero = jnp.zeros_like(q)
    k_blk = lambda j: k_ref[_key_rows(j, tk), :]
    vt_blk = lambda j: vt_ref[j]
    l1 = _flash_head(jnp.where(lane < DIFF_HEAD_DIM, q, zero), k_blk, vt_blk, sa_ref, sb_ref, acc1_ref, i)
    l2 = _flash_head(jnp.where(lane >= DIFF_HEAD_DIM, q, zero), k_blk, vt_blk, sa_ref, sb_ref, acc2_ref, i)

    lam = (jnp.exp(jnp.sum(lq1_ref[...] * lk1_ref[...], axis=-1, keepdims=True))
           - jnp.exp(jnp.sum(lq2_ref[...] * lk2_ref[...], axis=-1, keepdims=True)) + lambda_init)
    d = acc1_ref[...] / l1 - lam * (acc2_ref[...] / l2)
    dn = d * lax.rsqrt(jnp.mean(d * d, axis=0, keepdims=True) + NORM_EPS)
    o_ref[...] = (dn.T * gsub_ref[...] * (1.0 - lambda_init)).astype(o_ref.dtype)


def _diff_attn_call(lq1, lk1, lq2, lk2, gsub, q, k, vt, *, lambda_init):
    B, S, _ = q.shape
    tq, tk = QUERY_TILE, TOKEN_TILE
    tile = pl.BlockSpec((None, tq, LANES), lambda b, h, i: (b, i, h))
    return pl.pallas_call(
        functools.partial(_diff_attn_kernel, lambda_init=lambda_init),
        grid=(B, DIFF_HEADS, S // tq),
        in_specs=[_const_spec(a.shape) for a in (lq1, lk1, lq2, lk2, gsub)] + [
            tile,
            pl.BlockSpec((None, S, LANES), lambda b, h, i: (b, 0, h)),
            pl.BlockSpec((None, S // tk, LANES, tk), lambda b, h, i: (b, 0, h, 0))],
        out_specs=tile,
        out_shape=jax.ShapeDtypeStruct((B, S, DIFF_HEADS * DIFF_V_DIM), BF16),
        scratch_shapes=_attn_scratch(tq, tk, 2),
        compiler_params=_params(3),
        name="diff_attn",
    )(lq1, lk1, lq2, lk2, gsub, q, k, vt)


def _post_kernel(x_ref, om_ref, od_ref, gate_ref, wbm_ref, wbd_ref, wo_ref, gffn_ref, wg_ref, wu_ref, wd_ref,
                 gfin_ref, out_ref):
    d_model = x_ref.shape[1]
    gates = gate_ref[...].astype(F32)
    merged = (gates[:, :d_model] * _dot(om_ref[...], wbm_ref[...])
              + gates[:, d_model:] * _dot(od_ref[...], wbd_ref[...]))
    x = x_ref[...] + _dot(merged.astype(BF16), wo_ref[...])
    hn = _rms(x, gffn_ref[...]).astype(BF16)
    a = _dot(hn, wg_ref[...])
    hid = (a * (1.0 / (1.0 + jnp.exp(-a)))) * _dot(hn, wu_ref[...])
    x = x + _dot(hid.astype(BF16), wd_ref[...])
    out_ref[...] = _rms(x, gfin_ref[...])


def _post_call(x2, om, od, gates, wbm, wbd, wo, gffn, wg, wu, wd, gfin):
    T, D = x2.shape
    tm = TOKEN_TILE
    row = lambda a: pl.BlockSpec((tm, a.shape[1]), lambda i: (i, 0))
    consts = (wbm, wbd, wo, gffn, wg, wu, wd, gfin)
    return pl.pallas_call(
        _post_kernel,
        grid=(T // tm,),
        in_specs=[row(a) for a in (x2, om, od, gates)] + [_const_spec(a.shape) for a in consts],
        out_specs=pl.BlockSpec((tm, D), lambda i: (i, 0)),
        out_shape=jax.ShapeDtypeStruct((T, D), F32),
        compiler_params=_params(1),
        name="post",
    )(x2, om, od, gates, *consts)


def _rope_consts():
    lane = jnp.arange(LANES)

    def rows(r, rot_dim):
        half = rot_dim // 2
        inv_freq = jnp.exp(-math.log(ROPE_THETA) * jnp.arange(half, dtype=F32) * (2.0 / rot_dim))
        in_rot = (r >= 0) & (r < rot_dim)
        freq = jnp.where(in_rot, inv_freq[jnp.clip(r, 0, rot_dim - 1) % half], 0.0)
        lo = jnp.where((r >= 0) & (r < half), -1.0, 0.0)
        hi = jnp.where((r >= half) & (r < rot_dim), 1.0, 0.0)
        return [freq, lo, hi]

    z = jnp.zeros((LANES,), F32)
    return jnp.stack(rows(lane - MLA_NOPE_DIM, MLA_ROPE_DIM) + rows(lane % DIFF_HEAD_DIM, DIFF_ROT_DIM)
                     + [z, z]).astype(F32)


def _pad_last(a, width):
    return jnp.pad(a, [(0, 0)] * (a.ndim - 1) + [(0, width - a.shape[-1])])


def kernel(x, positions, norm_mix_g, w_in, b_gate, mla_q_norm_g, mla_w_uq, mla_kv_norm_g, mla_w_ukv, diff_lambda_q1, diff_lambda_k1, diff_lambda_q2, diff_lambda_k2, diff_subln_g, w_branch_mla, w_branch_diff, w_out, norm_ffn_g, w_ffn_gate, w_ffn_up, w_ffn_down, norm_final_g):
    B, S, D = x.shape
    T = B * S
    assert w_in.shape[0] == 1, "single-layer block"
    assert S % QUERY_TILE == 0
    layer = 0
    lambda_init = 0.8 - 0.6 * math.exp(-0.3 * layer)
    row = lambda a: a.reshape(1, -1)

    w = w_in[layer]
    o_kv = MLA_Q_RANK + MLA_KV_RANK
    o_kr = o_kv + MLA_ROPE_DIM
    o_dv = o_kr + 2 * DIFF_HEADS * LANES
    o_gate = o_dv + DIFF_HEADS * DIFF_V_DIM
    k_rope_w = jnp.pad(w[:, o_kv:o_kr], ((0, 0), (MLA_NOPE_DIM, LANES - MLA_QK_DIM)))
    w1 = jnp.concatenate([w[:, :o_kv], k_rope_w, w[:, o_kr:o_dv], w[:, o_gate:]], axis=1).astype(BF16)
    wdvt = w[:, o_dv:o_gate].T.astype(BF16)
    wuq = _pad_last(mla_w_uq[layer].reshape(MLA_Q_RANK, MLA_HEADS, MLA_QK_DIM), LANES)
    wuq = wuq.reshape(MLA_Q_RANK, MLA_HEADS * LANES).astype(BF16)
    wukv = mla_w_ukv[layer].reshape(MLA_KV_RANK, MLA_HEADS, MLA_NOPE_DIM + MLA_V_DIM)
    wuk = _pad_last(wukv[..., :MLA_NOPE_DIM], LANES).reshape(MLA_KV_RANK, MLA_HEADS * LANES).astype(BF16)
    wuvt = wukv[..., MLA_NOPE_DIM:].reshape(MLA_KV_RANK, MLA_HEADS * MLA_V_DIM).T.astype(BF16)

    q, k, vt, dq, dk, dvt, gates = _proj_call(
        x, positions.astype(F32).reshape(B, S, 1), _rope_consts(), row(norm_mix_g[layer]), w1,
        row(mla_q_norm_g[layer]), wuq, row(mla_kv_norm_g[layer]), wuk, wuvt, wdvt, row(b_gate[layer]))

    o_mla = _mla_attn_call(q, k, vt)
    o_diff = _diff_attn_call(row(diff_lambda_q1[layer]), row(diff_lambda_k1[layer]), row(diff_lambda_q2[layer]),
                             row(diff_lambda_k2[layer]), row(diff_subln_g[layer]), dq, dk, dvt,
                             lambda_init=lambda_init)

    out = _post_call(x.reshape(T, D), o_mla.reshape(T, -1), o_diff.reshape(T, -1), gates.reshape(T, -1),
                     w_branch_mla[layer].astype(BF16), w_branch_diff[layer].astype(BF16), w_out[layer].astype(BF16),
                     row(norm_ffn_g[layer]), w_ffn_gate[layer].astype(BF16), w_ffn_up[layer].astype(BF16),
                     w_ffn_down[layer].astype(BF16), row(norm_final_g))
    return out.reshape(B, S, D)
```

```python
import functools
import math
from typing import Callable, NamedTuple

import jax
import jax.numpy as jnp
from jax import lax
from jax.experimental import pallas as pl
from jax.experimental.pallas import tpu as pltpu

F32 = jnp.float32
BF16 = jnp.bfloat16

ROPE_THETA = 500000.0
NORM_EPS = 1e-6
LANES = 128
LOG2E = math.log2(math.e)

MLA_HEADS = 8
MLA_NOPE_DIM = 64
MLA_ROPE_DIM = 32
MLA_QK_DIM = MLA_NOPE_DIM + MLA_ROPE_DIM
MLA_V_DIM = 64
MLA_Q_RANK = 384
MLA_KV_RANK = 256
DIFF_HEADS = 4
DIFF_HEAD_DIM = 64
DIFF_V_DIM = 2 * DIFF_HEAD_DIM
DIFF_ROT_DIM = DIFF_HEAD_DIM // 4

VMEM_LIMIT_BYTES = 48 * 1024 * 1024
POST_TILE = 256
KEY_BLOCK = 256
RING = 4
QUERY_TILE = RING * KEY_BLOCK

_C_QLAT = 0
_C_KVLAT = _C_QLAT + MLA_Q_RANK
_C_KPE = _C_KVLAT + MLA_KV_RANK
_C_DQ = _C_KPE + LANES
_C_DK = _C_DQ + DIFF_HEADS * LANES
_C_GATE = _C_DK + DIFF_HEADS * LANES


def _rms(x, g):
    return x * lax.rsqrt(jnp.mean(x * x, axis=-1, keepdims=True) + NORM_EPS) * g


def _dot(a, b):
    return jnp.dot(a, b, preferred_element_type=F32)


def _dot_nt(a, b):
    return lax.dot_general(a, b, (((1,), (1,)), ((), ())), preferred_element_type=F32)


def _rope_slab(x, cos, sin_lo, sin_hi, half):
    return x * cos + pltpu.roll(x, LANES - half, 1) * sin_lo + pltpu.roll(x, half, 1) * sin_hi


def _const_spec(shape):
    return pl.BlockSpec(shape, lambda *_: (0,) * len(shape), pipeline_mode=pl.Buffered(1))


def _params(n_grid_dims):
    return pltpu.CompilerParams(dimension_semantics=("arbitrary",) * n_grid_dims, vmem_limit_bytes=VMEM_LIMIT_BYTES)


def _proj_kernel(x_ref, pos_ref, rc_ref, gmix_ref, w1_ref, gq_ref, wuq_ref, gkv_ref, wuk_ref, wuvt_ref, wdvt_ref,
                 bg_ref, q_out, k_out, vt_out, dq_out, dk_out, dvt_out, gate_out, *, mla_scale, diff_scale):
    x = x_ref[...]
    xn = _rms(x, gmix_ref[...]).astype(BF16)
    pos = pos_ref[...]

    ang = pos * rc_ref[0:1, :]
    cos_m, sin_m = jnp.cos(ang), jnp.sin(ang)
    sin_m_lo, sin_m_hi = sin_m * rc_ref[1:2, :], sin_m * rc_ref[2:3, :]
    ang = pos * rc_ref[3:4, :]
    cos_d, sin_d = jnp.cos(ang), jnp.sin(ang)
    sin_d_lo, sin_d_hi = sin_d * rc_ref[4:5, :], sin_d * rc_ref[5:6, :]

    q_lat = _dot(xn, w1_ref[:, _C_QLAT:_C_QLAT + MLA_Q_RANK])
    q = _dot(_rms(q_lat, gq_ref[...]).astype(BF16), wuq_ref[...])
    for h in range(MLA_HEADS):
        sl = slice(h * LANES, (h + 1) * LANES)
        q_out[:, sl] = (_rope_slab(q[:, sl], cos_m, sin_m_lo, sin_m_hi, MLA_ROPE_DIM // 2) * mla_scale).astype(BF16)

    kv_lat = _dot(xn, w1_ref[:, _C_KVLAT:_C_KVLAT + MLA_KV_RANK])
    kvn = _rms(kv_lat, gkv_ref[...]).astype(BF16)
    k_nope = _dot(kvn, wuk_ref[...])
    k_pe = _rope_slab(_dot(xn, w1_ref[:, _C_KPE:_C_KPE + LANES]), cos_m, sin_m_lo, sin_m_hi, MLA_ROPE_DIM // 2)
    for h in range(MLA_HEADS):
        sl = slice(h * LANES, (h + 1) * LANES)
        k_out[:, sl] = (k_nope[:, sl] + k_pe).astype(BF16)
    vt_out[...] = _dot_nt(wuvt_ref[...], kvn).astype(BF16)

    dq = _dot(xn, w1_ref[:, _C_DQ:_C_DQ + DIFF_HEADS * LANES])
    dk = _dot(xn, w1_ref[:, _C_DK:_C_DK + DIFF_HEADS * LANES])
    for h in range(DIFF_HEADS):
        sl = slice(h * LANES, (h + 1) * LANES)
        dq_out[:, sl] = (_rope_slab(dq[:, sl], cos_d, sin_d_lo, sin_d_hi, DIFF_ROT_DIM // 2) * diff_scale).astype(BF16)
        dk_out[:, sl] = _rope_slab(dk[:, sl], cos_d, sin_d_lo, sin_d_hi, DIFF_ROT_DIM // 2).astype(BF16)
    dvt_out[...] = _dot_nt(wdvt_ref[...], xn).astype(BF16)

    gate_pre = _dot(xn, w1_ref[:, _C_GATE:]) + bg_ref[...]
    gate_out[...] = (1.0 / (1.0 + jnp.exp(-gate_pre))).astype(BF16)


def _proj_call(x, pos, rc, gmix, w1, gq, wuq, gkv, wuk, wuvt, wdvt, bg):
    B, S, D = x.shape
    tm = KEY_BLOCK
    row = lambda w: pl.BlockSpec((None, tm, w), lambda b, i: (b, i, 0))
    col = lambda w: pl.BlockSpec((None, None, w, tm), lambda b, i: (b, i, 0, 0))
    consts = (rc, gmix, w1, gq, wuq, gkv, wuk, wuvt, wdvt, bg)
    row_w = lambda w: jax.ShapeDtypeStruct((B, S, w), BF16)
    col_w = lambda w: jax.ShapeDtypeStruct((B, S // tm, w, tm), BF16)
    wq, wv, wd = MLA_HEADS * LANES, MLA_HEADS * MLA_V_DIM, DIFF_HEADS * LANES
    return pl.pallas_call(
        functools.partial(_proj_kernel, mla_scale=MLA_QK_DIM ** -0.5 * LOG2E, diff_scale=DIFF_HEAD_DIM ** -0.5 * LOG2E),
        grid=(B, S // tm),
        in_specs=[row(D), row(1)] + [_const_spec(a.shape) for a in consts],
        out_specs=[row(wq), row(wq), col(wv), row(wd), row(wd), col(wd), row(bg.shape[1])],
        out_shape=[row_w(wq), row_w(wq), col_w(wv), row_w(wd), row_w(wd), col_w(wd), row_w(bg.shape[1])],
        compiler_params=_params(2),
        name="proj",
    )(x, pos, *consts)


class _Stream(NamedTuple):
    q: jax.Array
    k_blk: Callable
    vt_blk: Callable
    s_refs: tuple
    p_refs: tuple
    acc_ref: object


def _flash_streams(streams, i):
    tk, tq = streams[0].s_refs[0].shape

    def qk(st, j, r, c0=0):
        s = _dot_nt(st.k_blk(j), st.q[c0:, :])
        st.s_refs[r][:, c0:] = s
        return jnp.max(s, axis=0, keepdims=True)

    def softmax(st, r, col_max, m, l):
        m_new = jnp.maximum(m, col_max)
        p = jnp.exp2(st.s_refs[r][...] - m_new)
        st.p_refs[r][...] = p.astype(BF16)
        alpha = jnp.exp2(m - m_new)
        return m_new, alpha * l + jnp.sum(p, axis=0, keepdims=True), alpha

    def pv(st, j, r, c0=0):
        return _dot(st.vt_blk(j), st.p_refs[r][:, c0:])

    def start(st):
        st.acc_ref[...] = jnp.zeros(st.acc_ref.shape, F32)
        st.p_refs[2][...] = jnp.zeros((tk, tq), BF16)
        st.p_refs[3][...] = jnp.zeros((tk, tq), BF16)
        one = jnp.ones((1, tq), F32)
        return (jnp.full((1, tq), -jnp.inf, F32), jnp.zeros((1, tq), F32), one, one, qk(st, 0, 0), qk(st, 1, 1))

    def half(st, j0, r0, m, l, alphas_prev, maxes):
        r2 = (r0 + 2) % RING
        pv_lo = pv(st, jnp.maximum(j0 - 2, 0), r2)
        pv_hi = pv(st, jnp.maximum(j0 - 1, 0), r2 + 1)
        maxes_next = (qk(st, j0 + 2, r2), qk(st, j0 + 3, r2 + 1))
        m, l, alpha_lo = softmax(st, r0, maxes[0], m, l)
        m, l, alpha_hi = softmax(st, r0 + 1, maxes[1], m, l)
        st.acc_ref[...] = alphas_prev[1] * (alphas_prev[0] * st.acc_ref[...] + pv_lo) + pv_hi
        return m, l, (alpha_lo, alpha_hi), maxes_next

    def trip(st, t, carry):
        m, l, a2, a3, max0, max1 = carry
        j0 = RING * t
        m, l, (a0, a1), (max2, max3) = half(st, j0, 0, m, l, (a2, a3), (max0, max1))
        m, l, (a2, a3), (max0, max1) = half(st, j0 + 2, 2, m, l, (a0, a1), (max2, max3))
        return m, l, a2, a3, max0, max1

    def finish(st, carry):
        m, l, a2, a3, _, _ = carry
        j0 = RING * i
        pv_lo = pv(st, jnp.maximum(j0 - 2, 0), 2)
        pv_hi = pv(st, jnp.maximum(j0 - 1, 0), 3)
        qk(st, j0 + 2, 2, 2 * tk)
        qk(st, j0 + 3, 3, 3 * tk)
        st.acc_ref[...] = a3 * (a2 * st.acc_ref[...] + pv_lo) + pv_hi
        for d in range(RING):
            c0 = d * tk
            w = tq - c0
            mask = (lax.broadcasted_iota(jnp.int32, (tk, w), 0) <= lax.broadcasted_iota(jnp.int32, (tk, w), 1))
            s = jnp.where(mask, st.s_refs[d][:, c0:], -jnp.inf)
            m_old = m[:, c0:]
            m_new = jnp.maximum(m_old, jnp.max(s, axis=0, keepdims=True))
            p = jnp.exp2(s - m_new)
            st.p_refs[d][:, c0:] = p.astype(BF16)
            alpha = jnp.exp2(m_old - m_new)
            l_new = alpha * l[:, c0:] + jnp.sum(p, axis=0, keepdims=True)
            m = m_new if d == 0 else jnp.concatenate([m[:, :c0], m_new], axis=1)
            l = l_new if d == 0 else jnp.concatenate([l[:, :c0], l_new], axis=1)
            st.acc_ref[:, c0:] = alpha * st.acc_ref[:, c0:] + pv(st, j0 + d, d, c0)
        return l

    carries = lax.fori_loop(0, i, lambda t, cs: tuple(trip(st, t, c) for st, c in zip(streams, cs)),
                            tuple(start(st) for st in streams))
    return [finish(st, c) for st, c in zip(streams, carries)]


def _key_rows(j, tk):
    return pl.ds(pl.multiple_of(j * tk, tk), tk)


def _split_scratch(scratch, n_streams):
    n = n_streams * RING
    s, p = scratch[:n], scratch[n:2 * n]
    return [(s[k * RING:(k + 1) * RING], p[k * RING:(k + 1) * RING]) for k in range(n_streams)], scratch[2 * n:]


def _attn_scratch(tq, tk, n_streams, n_acc):
    return ([pltpu.VMEM((tk, tq), F32)] * (n_streams * RING) + [pltpu.VMEM((tk, tq), BF16)] * (n_streams * RING)
            + [pltpu.VMEM((LANES, tq), F32)] * n_acc)


def _mla_attn_kernel(q_ref, k_ref, vt_ref, o_ref, *scratch):
    i = pl.program_id(2)
    bufs, (acc_ref,) = _split_scratch(scratch, 2)
    tk = bufs[0][0][0].shape[0]
    streams = []
    for hh, (s_refs, p_refs) in enumerate(bufs):
        sl = slice(hh * LANES, (hh + 1) * LANES)
        rows = slice(hh * MLA_V_DIM, (hh + 1) * MLA_V_DIM)
        streams.append(_Stream(q_ref[:, sl], lambda j, sl=sl: k_ref[_key_rows(j, tk), sl],
                               lambda j, rows=rows: vt_ref[j, rows, :], s_refs, p_refs, acc_ref.at[rows]))
    outs = [st.acc_ref[...] / l for st, l in zip(streams, _flash_streams(streams, i))]
    o_ref[...] = jnp.concatenate(outs, axis=0).T.astype(o_ref.dtype)


def _mla_attn_call(q, k, vt):
    B, S, _ = q.shape
    tq, tk = QUERY_TILE, KEY_BLOCK
    return pl.pallas_call(
        _mla_attn_kernel,
        grid=(B, MLA_HEADS // 2, S // tq),
        in_specs=[pl.BlockSpec((None, tq, 2 * LANES), lambda b, h, i: (b, i, h)),
                  pl.BlockSpec((None, S, 2 * LANES), lambda b, h, i: (b, 0, h)),
                  pl.BlockSpec((None, S // tk, LANES, tk), lambda b, h, i: (b, 0, h, 0))],
        out_specs=pl.BlockSpec((None, tq, LANES), lambda b, h, i: (b, i, h)),
        out_shape=jax.ShapeDtypeStruct((B, S, MLA_HEADS * MLA_V_DIM), BF16),
        scratch_shapes=_attn_scratch(tq, tk, 2, 1),
        compiler_params=_params(3),
        name="mla_attn",
    )(q, k, vt)


def _diff_attn_kernel(lq1_ref, lk1_ref, lq2_ref, lk2_ref, gsub_ref, q_ref, k_ref, vt_ref, o_ref, *scratch,
                      lambda_init):
    i = pl.program_id(2)
    bufs, (acc1_ref, acc2_ref) = _split_scratch(scratch, 2)
    tk = bufs[0][0][0].shape[0]
    q = q_ref[...]
    lane = lax.broadcasted_iota(jnp.int32, q.shape, 1)
    zero = jnp.zeros_like(q)
    k_blk = lambda j: k_ref[_key_rows(j, tk), :]
    vt_blk = lambda j: vt_ref[j]
    l1, l2 = _flash_streams([
        _Stream(jnp.where(lane < DIFF_HEAD_DIM, q, zero), k_blk, vt_blk, *bufs[0], acc1_ref),
        _Stream(jnp.where(lane >= DIFF_HEAD_DIM, q, zero), k_blk, vt_blk, *bufs[1], acc2_ref)], i)

    lam = (jnp.exp(jnp.sum(lq1_ref[...] * lk1_ref[...], axis=-1, keepdims=True))
           - jnp.exp(jnp.sum(lq2_ref[...] * lk2_ref[...], axis=-1, keepdims=True)) + lambda_init)
    d = acc1_ref[...] / l1 - lam * (acc2_ref[...] / l2)
    dn = d * lax.rsqrt(jnp.mean(d * d, axis=0, keepdims=True) + NORM_EPS)
    o_ref[...] = (dn.T * gsub_ref[...] * (1.0 - lambda_init)).astype(o_ref.dtype)


def _diff_attn_call(lq1, lk1, lq2, lk2, gsub, q, k, vt, *, lambda_init):
    B, S, _ = q.shape
    tq, tk = QUERY_TILE, KEY_BLOCK
    tile = pl.BlockSpec((None, tq, LANES), lambda b, h, i: (b, i, h))
    return pl.pallas_call(
        functools.partial(_diff_attn_kernel, lambda_init=lambda_init),
        grid=(B, DIFF_HEADS, S // tq),
        in_specs=[_const_spec(a.shape) for a in (lq1, lk1, lq2, lk2, gsub)] + [
            tile,
            pl.BlockSpec((None, S, LANES), lambda b, h, i: (b, 0, h)),
            pl.BlockSpec((None, S // tk, LANES, tk), lambda b, h, i: (b, 0, h, 0))],
        out_specs=tile,
        out_shape=jax.ShapeDtypeStruct((B, S, DIFF_HEADS * DIFF_V_DIM), BF16),
        scratch_shapes=_attn_scratch(tq, tk, 2, 2),
        compiler_params=_params(3),
        name="diff_attn",
    )(lq1, lk1, lq2, lk2, gsub, q, k, vt)


def _post_kernel(x_ref, om_ref, od_ref, gate_ref, wbm_ref, wbd_ref, wo_ref, gffn_ref, wg_ref, wu_ref, wd_ref,
                 gfin_ref, out_ref):
    d_model = x_ref.shape[1]
    gates = gate_ref[...].astype(F32)
    merged = (gates[:, :d_model] * _dot(om_ref[...], wbm_ref[...])
              + gates[:, d_model:] * _dot(od_ref[...], wbd_ref[...]))
    x = x_ref[...] + _dot(merged.astype(BF16), wo_ref[...])
    hn = _rms(x, gffn_ref[...]).astype(BF16)
    a = _dot(hn, wg_ref[...])
    hid = (a * (1.0 / (1.0 + jnp.exp(-a)))) * _dot(hn, wu_ref[...])
    x = x + _dot(hid.astype(BF16), wd_ref[...])
    out_ref[...] = _rms(x, gfin_ref[...])


def _post_call(x2, om, od, gates, wbm, wbd, wo, gffn, wg, wu, wd, gfin):
    T, D = x2.shape
    tm = POST_TILE
    row = lambda a: pl.BlockSpec((tm, a.shape[1]), lambda i: (i, 0))
    consts = (wbm, wbd, wo, gffn, wg, wu, wd, gfin)
    return pl.pallas_call(
        _post_kernel,
        grid=(T // tm,),
        in_specs=[row(a) for a in (x2, om, od, gates)] + [_const_spec(a.shape) for a in consts],
        out_specs=pl.BlockSpec((tm, D), lambda i: (i, 0)),
        out_shape=jax.ShapeDtypeStruct((T, D), F32),
        compiler_params=_params(1),
        name="post",
    )(x2, om, od, gates, *consts)


def _rope_consts():
    lane = jnp.arange(LANES)

    def rows(r, rot_dim):
        half = rot_dim // 2
        inv_freq = jnp.exp(-math.log(ROPE_THETA) * jnp.arange(half, dtype=F32) * (2.0 / rot_dim))
        in_rot = (r >= 0) & (r < rot_dim)
        freq = jnp.where(in_rot, inv_freq[jnp.clip(r, 0, rot_dim - 1) % half], 0.0)
        lo = jnp.where((r >= 0) & (r < half), -1.0, 0.0)
        hi = jnp.where((r >= half) & (r < rot_dim), 1.0, 0.0)
        return [freq, lo, hi]

    z = jnp.zeros((LANES,), F32)
    return jnp.stack(rows(lane - MLA_NOPE_DIM, MLA_ROPE_DIM) + rows(lane % DIFF_HEAD_DIM, DIFF_ROT_DIM)
                     + [z, z]).astype(F32)


def _pad_last(a, width):
    return jnp.pad(a, [(0, 0)] * (a.ndim - 1) + [(0, width - a.shape[-1])])


def kernel(x, positions, norm_mix_g, w_in, b_gate, mla_q_norm_g, mla_w_uq, mla_kv_norm_g, mla_w_ukv, diff_lambda_q1, diff_lambda_k1, diff_lambda_q2, diff_lambda_k2, diff_subln_g, w_branch_mla, w_branch_diff, w_out, norm_ffn_g, w_ffn_gate, w_ffn_up, w_ffn_down, norm_final_g):
    B, S, D = x.shape
    T = B * S
    assert w_in.shape[0] == 1, "single-layer block"
    assert S % QUERY_TILE == 0
    layer = 0
    lambda_init = 0.8 - 0.6 * math.exp(-0.3 * layer)
    row = lambda a: a.reshape(1, -1)

    w = w_in[layer]
    o_kv = MLA_Q_RANK + MLA_KV_RANK
    o_kr = o_kv + MLA_ROPE_DIM
    o_dv = o_kr + 2 * DIFF_HEADS * LANES
    o_gate = o_dv + DIFF_HEADS * DIFF_V_DIM
    k_rope_w = jnp.pad(w[:, o_kv:o_kr], ((0, 0), (MLA_NOPE_DIM, LANES - MLA_QK_DIM)))
    w1 = jnp.concatenate([w[:, :o_kv], k_rope_w, w[:, o_kr:o_dv], w[:, o_gate:]], axis=1).astype(BF16)
    wdvt = w[:, o_dv:o_gate].T.astype(BF16)
    wuq = _pad_last(mla_w_uq[layer].reshape(MLA_Q_RANK, MLA_HEADS, MLA_QK_DIM), LANES)
    wuq = wuq.reshape(MLA_Q_RANK, MLA_HEADS * LANES).astype(BF16)
    wukv = mla_w_ukv[layer].reshape(MLA_KV_RANK, MLA_HEADS, MLA_NOPE_DIM + MLA_V_DIM)
    wuk = _pad_last(wukv[..., :MLA_NOPE_DIM], LANES).reshape(MLA_KV_RANK, MLA_HEADS * LANES).astype(BF16)
    wuvt = wukv[..., MLA_NOPE_DIM:].reshape(MLA_KV_RANK, MLA_HEADS * MLA_V_DIM).T.astype(BF16)

    q, k, vt, dq, dk, dvt, gates = _proj_call(
        x, positions.astype(F32).reshape(B, S, 1), _rope_consts(), row(norm_mix_g[layer]), w1,
        row(mla_q_norm_g[layer]), wuq, row(mla_kv_norm_g[layer]), wuk, wuvt, wdvt, row(b_gate[layer]))

    o_mla = _mla_attn_call(q, k, vt)
    o_diff = _diff_attn_call(row(diff_lambda_q1[layer]), row(diff_lambda_k1[layer]), row(diff_lambda_q2[layer]),
                             row(diff_lambda_k2[layer]), row(diff_subln_g[layer]), dq, dk, dvt,
                             lambda_init=lambda_init)

    out = _post_call(x.reshape(T, D), o_mla.reshape(T, -1), o_diff.reshape(T, -1), gates.reshape(T, -1),
                     w_branch_mla[layer].astype(BF16), w_branch_diff[layer].astype(BF16), w_out[layer].astype(BF16),
                     row(norm_ffn_g[layer]), w_ffn_gate[layer].astype(BF16), w_ffn_up[layer].astype(BF16),
                     w_ffn_down[layer].astype(BF16), row(norm_final_g))
    return out.reshape(B, S, D)
```

```python
import functools
import math
from typing import Callable, NamedTuple

import jax
import jax.numpy as jnp
from jax import lax
from jax.experimental import pallas as pl
from jax.experimental.pallas import tpu as pltpu

F32 = jnp.float32
BF16 = jnp.bfloat16

ROPE_THETA = 500000.0
NORM_EPS = 1e-6
LANES = 128
LOG2E = math.log2(math.e)

MLA_HEADS = 8
MLA_NOPE_DIM = 64
MLA_ROPE_DIM = 32
MLA_QK_DIM = MLA_NOPE_DIM + MLA_ROPE_DIM
MLA_V_DIM = 64
MLA_Q_RANK = 384
MLA_KV_RANK = 256
DIFF_HEADS = 4
DIFF_HEAD_DIM = 64
DIFF_V_DIM = 2 * DIFF_HEAD_DIM
DIFF_ROT_DIM = DIFF_HEAD_DIM // 4

VMEM_LIMIT_BYTES = 48 * 1024 * 1024
POST_TILE = 256
KEY_BLOCK = 256
RING = 4
SUM_ROWS = 16
QUERY_TILE = RING * KEY_BLOCK

_C_QLAT = 0
_C_KVLAT = _C_QLAT + MLA_Q_RANK
_C_GATE = _C_KVLAT + MLA_KV_RANK


def _rms(x, g):
    return x * lax.rsqrt(jnp.mean(x * x, axis=-1, keepdims=True) + NORM_EPS) * g


def _dot(a, b):
    return jnp.dot(a, b, preferred_element_type=F32)


def _dot_nt(a, b):
    return lax.dot_general(a, b, (((1,), (1,)), ((), ())), preferred_element_type=F32)


def _rope_rows(xt, base, half, cos, sin):
    x1, x2 = xt[base:base + half], xt[base + half:base + 2 * half]
    return [x1 * cos - x2 * sin, x2 * cos + x1 * sin]


def _const_spec(shape):
    return pl.BlockSpec(shape, lambda *_: (0,) * len(shape), pipeline_mode=pl.Buffered(1))


def _params(n_grid_dims):
    return pltpu.CompilerParams(dimension_semantics=("arbitrary",) * n_grid_dims, vmem_limit_bytes=VMEM_LIMIT_BYTES)


def _proj_kernel(x_ref, pos_ref, fm_ref, fd_ref, gmix_ref, w1_ref, gq_ref, wuqt_ref, gkv_ref, wuk_ref, wuvt_ref,
                 wkpet_ref, wdqt_ref, wdkt_ref, wdvt_ref, bg_ref,
                 qt_out, k_out, vt_out, dqt_out, dk_out, dvt_out, gate_out, *, mla_scale, diff_scale):
    x = x_ref[...]
    xn = _rms(x, gmix_ref[...]).astype(BF16)
    pos = pos_ref[...]
    ang = fm_ref[...] * pos
    cos_m, sin_m = jnp.cos(ang), jnp.sin(ang)
    ang = fd_ref[...] * pos
    cos_d, sin_d = jnp.cos(ang), jnp.sin(ang)
    half_m, half_d = MLA_ROPE_DIM // 2, DIFF_ROT_DIM // 2

    q_lat = _dot(xn, w1_ref[:, _C_QLAT:_C_QLAT + MLA_Q_RANK])
    qt = _dot_nt(wuqt_ref[...], _rms(q_lat, gq_ref[...]).astype(BF16))
    for h in range(MLA_HEADS):
        b0 = h * LANES
        slab = jnp.concatenate([qt[b0:b0 + MLA_NOPE_DIM]] + _rope_rows(qt, b0 + MLA_NOPE_DIM, half_m, cos_m, sin_m)
                               + [qt[b0 + MLA_QK_DIM:b0 + LANES]], axis=0)
        qt_out[b0:b0 + LANES, :] = (slab * mla_scale).astype(BF16)

    kv_lat = _dot(xn, w1_ref[:, _C_KVLAT:_C_KVLAT + MLA_KV_RANK])
    kvn = _rms(kv_lat, gkv_ref[...]).astype(BF16)
    k_nope = _dot(kvn, wuk_ref[...])
    kpt = _dot_nt(wkpet_ref[...], xn)
    k_pe = jnp.concatenate([kpt[:MLA_NOPE_DIM]] + _rope_rows(kpt, MLA_NOPE_DIM, half_m, cos_m, sin_m)
                           + [kpt[MLA_QK_DIM:]], axis=0).T
    for h in range(MLA_HEADS):
        sl = slice(h * LANES, (h + 1) * LANES)
        k_out[:, sl] = (k_nope[:, sl] + k_pe).astype(BF16)
    vt_out[...] = _dot_nt(wuvt_ref[...], kvn).astype(BF16)

    def diff_rope(xt, h):
        parts = []
        for c in range(2):
            base = h * LANES + c * DIFF_HEAD_DIM
            parts += _rope_rows(xt, base, half_d, cos_d, sin_d) + [xt[base + DIFF_ROT_DIM:base + DIFF_HEAD_DIM]]
        return jnp.concatenate(parts, axis=0)

    dqt = _dot_nt(wdqt_ref[...], xn)
    dkt = _dot_nt(wdkt_ref[...], xn)
    for h in range(DIFF_HEADS):
        sl = slice(h * LANES, (h + 1) * LANES)
        dqt_out[sl, :] = (diff_rope(dqt, h) * diff_scale).astype(BF16)
        dk_out[:, sl] = diff_rope(dkt, h).T.astype(BF16)
    dvt_out[...] = _dot_nt(wdvt_ref[...], xn).astype(BF16)

    gate_pre = _dot(xn, w1_ref[:, _C_GATE:]) + bg_ref[...]
    gate_out[...] = (1.0 / (1.0 + jnp.exp(-gate_pre))).astype(BF16)


def _proj_call(x, pos, consts):
    B, S, D = x.shape
    tm = KEY_BLOCK
    n_gate = consts[-1].shape[1]
    row = lambda w: pl.BlockSpec((None, tm, w), lambda b, i: (b, i, 0))
    feat = lambda w: pl.BlockSpec((None, w, tm), lambda b, i: (b, 0, i))
    col = lambda w: pl.BlockSpec((None, None, w, tm), lambda b, i: (b, i, 0, 0))
    row_w = lambda w: jax.ShapeDtypeStruct((B, S, w), BF16)
    feat_w = lambda w: jax.ShapeDtypeStruct((B, w, S), BF16)
    col_w = lambda w: jax.ShapeDtypeStruct((B, S // tm, w, tm), BF16)
    wq, wv, wd = MLA_HEADS * LANES, MLA_HEADS * MLA_V_DIM, DIFF_HEADS * LANES
    return pl.pallas_call(
        functools.partial(_proj_kernel, mla_scale=MLA_QK_DIM ** -0.5 * LOG2E, diff_scale=DIFF_HEAD_DIM ** -0.5 * LOG2E),
        grid=(B, S // tm),
        in_specs=[row(D), feat(1)] + [_const_spec(a.shape) for a in consts],
        out_specs=[feat(wq), row(wq), col(wv), feat(wd), row(wd), col(wd), row(n_gate)],
        out_shape=[feat_w(wq), row_w(wq), col_w(wv), feat_w(wd), row_w(wd), col_w(wd), row_w(n_gate)],
        compiler_params=_params(2),
        name="proj",
    )(x, pos, *consts)


class _Stream(NamedTuple):
    qt: jax.Array
    k_blk: Callable
    vt_blk: Callable
    s_refs: tuple
    p_refs: tuple
    acc_ref: object


def _flash_streams(streams, i):
    tk, tq = streams[0].s_refs[0].shape

    def qk(st, j, r, c0=0):
        s = _dot(st.k_blk(j), st.qt[:, c0:])
        st.s_refs[r][:, c0:] = s
        return jnp.max(s, axis=0, keepdims=True)

    def softmax(st, r, col_max, m):
        m_new = jnp.maximum(m, col_max)
        st.p_refs[r][...] = jnp.exp2(st.s_refs[r][...] - m_new).astype(BF16)
        return m_new, jnp.exp2(m - m_new)

    ones = jnp.ones((SUM_ROWS, tk), BF16)

    def pv(st, j, r, c0=0):
        return _dot(jnp.concatenate([st.vt_blk(j), ones], axis=0), st.p_refs[r][:, c0:])

    def start(st):
        st.acc_ref[...] = jnp.zeros(st.acc_ref.shape, F32)
        st.p_refs[2][...] = jnp.zeros((tk, tq), BF16)
        st.p_refs[3][...] = jnp.zeros((tk, tq), BF16)
        one = jnp.ones((1, tq), F32)
        return jnp.full((1, tq), -jnp.inf, F32), one, one, qk(st, 0, 0), qk(st, 1, 1)

    def half(st, j0, r0, m, alphas_prev, maxes):
        r2 = (r0 + 2) % RING
        pv_lo = pv(st, jnp.maximum(j0 - 2, 0), r2)
        pv_hi = pv(st, jnp.maximum(j0 - 1, 0), r2 + 1)
        maxes_next = (qk(st, j0 + 2, r2), qk(st, j0 + 3, r2 + 1))
        m, alpha_lo = softmax(st, r0, maxes[0], m)
        m, alpha_hi = softmax(st, r0 + 1, maxes[1], m)
        st.acc_ref[...] = alphas_prev[1] * (alphas_prev[0] * st.acc_ref[...] + pv_lo) + pv_hi
        return m, (alpha_lo, alpha_hi), maxes_next

    def trip(st, t, carry):
        m, a2, a3, max0, max1 = carry
        j0 = RING * t
        m, (a0, a1), (max2, max3) = half(st, j0, 0, m, (a2, a3), (max0, max1))
        m, (a2, a3), (max0, max1) = half(st, j0 + 2, 2, m, (a0, a1), (max2, max3))
        return m, a2, a3, max0, max1

    def finish(st, carry):
        m, a2, a3, _, _ = carry
        j0 = RING * i
        pv_lo = pv(st, jnp.maximum(j0 - 2, 0), 2)
        pv_hi = pv(st, jnp.maximum(j0 - 1, 0), 3)
        qk(st, j0 + 2, 2, 2 * tk)
        qk(st, j0 + 3, 3, 3 * tk)
        st.acc_ref[...] = a3 * (a2 * st.acc_ref[...] + pv_lo) + pv_hi
        for d in range(RING):
            c0 = d * tk
            w = tq - c0
            mask = (lax.broadcasted_iota(jnp.int32, (tk, w), 0) <= lax.broadcasted_iota(jnp.int32, (tk, w), 1))
            s = jnp.where(mask, st.s_refs[d][:, c0:], -jnp.inf)
            m_old = m[:, c0:]
            m_new = jnp.maximum(m_old, jnp.max(s, axis=0, keepdims=True))
            st.p_refs[d][:, c0:] = jnp.exp2(s - m_new).astype(BF16)
            alpha = jnp.exp2(m_old - m_new)
            m = m_new if d == 0 else jnp.concatenate([m[:, :c0], m_new], axis=1)
            st.acc_ref[:, c0:] = alpha * st.acc_ref[:, c0:] + pv(st, j0 + d, d, c0)

    carries = lax.fori_loop(0, i, lambda t, cs: tuple(trip(st, t, c) for st, c in zip(streams, cs)),
                            tuple(start(st) for st in streams))
    for st, c in zip(streams, carries):
        finish(st, c)


def _normalised(acc_ref, dv):
    return acc_ref[:dv, :] / acc_ref[dv:dv + 1, :]


def _key_rows(j, tk):
    return pl.ds(pl.multiple_of(j * tk, tk), tk)


def _split_scratch(scratch, n_streams):
    n = n_streams * RING
    s, p = scratch[:n], scratch[n:2 * n]
    return [(s[k * RING:(k + 1) * RING], p[k * RING:(k + 1) * RING]) for k in range(n_streams)], scratch[2 * n:]


def _attn_scratch(tq, tk, dv):
    return ([pltpu.VMEM((tk, tq), F32)] * (2 * RING) + [pltpu.VMEM((tk, tq), BF16)] * (2 * RING)
            + [pltpu.VMEM((dv + SUM_ROWS, tq), F32)] * 2)


def _mla_attn_kernel(qt_ref, k_ref, vt_ref, o_ref, *scratch):
    i = pl.program_id(2)
    bufs, accs = _split_scratch(scratch, 2)
    tk = bufs[0][0][0].shape[0]
    streams = []
    for hh, ((s_refs, p_refs), acc_ref) in enumerate(zip(bufs, accs)):
        sl = slice(hh * LANES, (hh + 1) * LANES)
        rows = slice(hh * MLA_V_DIM, (hh + 1) * MLA_V_DIM)
        streams.append(_Stream(qt_ref[sl, :], lambda j, sl=sl: k_ref[_key_rows(j, tk), sl],
                               lambda j, rows=rows: vt_ref[j, rows, :], s_refs, p_refs, acc_ref))
    _flash_streams(streams, i)
    outs = [_normalised(acc_ref, MLA_V_DIM) for acc_ref in accs]
    o_ref[...] = jnp.concatenate(outs, axis=0).T.astype(o_ref.dtype)


def _mla_attn_call(qt, k, vt):
    B, S, _ = k.shape
    tq, tk = QUERY_TILE, KEY_BLOCK
    return pl.pallas_call(
        _mla_attn_kernel,
        grid=(B, MLA_HEADS // 2, S // tq),
        in_specs=[pl.BlockSpec((None, 2 * LANES, tq), lambda b, h, i: (b, h, i)),
                  pl.BlockSpec((None, S, 2 * LANES), lambda b, h, i: (b, 0, h)),
                  pl.BlockSpec((None, S // tk, LANES, tk), lambda b, h, i: (b, 0, h, 0))],
        out_specs=pl.BlockSpec((None, tq, LANES), lambda b, h, i: (b, i, h)),
        out_shape=jax.ShapeDtypeStruct((B, S, MLA_HEADS * MLA_V_DIM), BF16),
        scratch_shapes=_attn_scratch(tq, tk, MLA_V_DIM),
        compiler_params=_params(3),
        name="mla_attn",
    )(qt, k, vt)


def _diff_attn_kernel(lq1_ref, lk1_ref, lq2_ref, lk2_ref, gsub_ref, qt_ref, k_ref, vt_ref, o_ref, *scratch,
                      lambda_init):
    i = pl.program_id(2)
    bufs, (acc1_ref, acc2_ref) = _split_scratch(scratch, 2)
    tk = bufs[0][0][0].shape[0]
    qt = qt_ref[...]
    feat = lax.broadcasted_iota(jnp.int32, qt.shape, 0)
    zero = jnp.zeros_like(qt)
    k_blk = lambda j: k_ref[_key_rows(j, tk), :]
    vt_blk = lambda j: vt_ref[j]
    _flash_streams([
        _Stream(jnp.where(feat < DIFF_HEAD_DIM, qt, zero), k_blk, vt_blk, *bufs[0], acc1_ref),
        _Stream(jnp.where(feat >= DIFF_HEAD_DIM, qt, zero), k_blk, vt_blk, *bufs[1], acc2_ref)], i)

    lam = (jnp.exp(jnp.sum(lq1_ref[...] * lk1_ref[...], axis=-1, keepdims=True))
           - jnp.exp(jnp.sum(lq2_ref[...] * lk2_ref[...], axis=-1, keepdims=True)) + lambda_init)
    d = _normalised(acc1_ref, DIFF_V_DIM) - lam * _normalised(acc2_ref, DIFF_V_DIM)
    dn = d * lax.rsqrt(jnp.mean(d * d, axis=0, keepdims=True) + NORM_EPS)
    o_ref[...] = (dn.T * gsub_ref[...] * (1.0 - lambda_init)).astype(o_ref.dtype)


def _diff_attn_call(lq1, lk1, lq2, lk2, gsub, qt, k, vt, *, lambda_init):
    B, S, _ = k.shape
    tq, tk = QUERY_TILE, KEY_BLOCK
    tile = pl.BlockSpec((None, tq, LANES), lambda b, h, i: (b, i, h))
    return pl.pallas_call(
        functools.partial(_diff_attn_kernel, lambda_init=lambda_init),
        grid=(B, DIFF_HEADS, S // tq),
        in_specs=[_const_spec(a.shape) for a in (lq1, lk1, lq2, lk2, gsub)] + [
            pl.BlockSpec((None, LANES, tq), lambda b, h, i: (b, h, i)),
            pl.BlockSpec((None, S, LANES), lambda b, h, i: (b, 0, h)),
            pl.BlockSpec((None, S // tk, LANES, tk), lambda b, h, i: (b, 0, h, 0))],
        out_specs=tile,
        out_shape=jax.ShapeDtypeStruct((B, S, DIFF_HEADS * DIFF_V_DIM), BF16),
        scratch_shapes=_attn_scratch(tq, tk, DIFF_V_DIM),
        compiler_params=_params(3),
        name="diff_attn",
    )(lq1, lk1, lq2, lk2, gsub, qt, k, vt)


def _post_kernel(x_ref, om_ref, od_ref, gate_ref, wbm_ref, wbd_ref, wo_ref, gffn_ref, wg_ref, wu_ref, wd_ref,
                 gfin_ref, out_ref):
    d_model = x_ref.shape[1]
    gates = gate_ref[...].astype(F32)
    merged = (gates[:, :d_model] * _dot(om_ref[...], wbm_ref[...])
              + gates[:, d_model:] * _dot(od_ref[...], wbd_ref[...]))
    x = x_ref[...] + _dot(merged.astype(BF16), wo_ref[...])
    hn = _rms(x, gffn_ref[...]).astype(BF16)
    a = _dot(hn, wg_ref[...])
    hid = (a * (1.0 / (1.0 + jnp.exp(-a)))) * _dot(hn, wu_ref[...])
    x = x + _dot(hid.astype(BF16), wd_ref[...])
    out_ref[...] = _rms(x, gfin_ref[...])


def _post_call(x2, om, od, gates, wbm, wbd, wo, gffn, wg, wu, wd, gfin):
    T, D = x2.shape
    tm = POST_TILE
    row = lambda a: pl.BlockSpec((tm, a.shape[1]), lambda i: (i, 0))
    consts = (wbm, wbd, wo, gffn, wg, wu, wd, gfin)
    return pl.pallas_call(
        _post_kernel,
        grid=(T // tm,),
        in_specs=[row(a) for a in (x2, om, od, gates)] + [_const_spec(a.shape) for a in consts],
        out_specs=pl.BlockSpec((tm, D), lambda i: (i, 0)),
        out_shape=jax.ShapeDtypeStruct((T, D), F32),
        compiler_params=_params(1),
        name="post",
    )(x2, om, od, gates, *consts)


def _inv_freq_column(rot_dim):
    half = rot_dim // 2
    return jnp.exp(-math.log(ROPE_THETA) * jnp.arange(half, dtype=F32) * (2.0 / rot_dim)).reshape(half, 1)


def _pad_last(a, width):
    return jnp.pad(a, [(0, 0)] * (a.ndim - 1) + [(0, width - a.shape[-1])])


def kernel(x, positions, norm_mix_g, w_in, b_gate, mla_q_norm_g, mla_w_uq, mla_kv_norm_g, mla_w_ukv, diff_lambda_q1, diff_lambda_k1, diff_lambda_q2, diff_lambda_k2, diff_subln_g, w_branch_mla, w_branch_diff, w_out, norm_ffn_g, w_ffn_gate, w_ffn_up, w_ffn_down, norm_final_g):
    B, S, D = x.shape
    T = B * S
    assert w_in.shape[0] == 1, "single-layer block"
    assert S % QUERY_TILE == 0
    layer = 0
    lambda_init = 0.8 - 0.6 * math.exp(-0.3 * layer)
    row = lambda a: a.reshape(1, -1)

    w = w_in[layer]
    o_kv = MLA_Q_RANK + MLA_KV_RANK
    o_kr = o_kv + MLA_ROPE_DIM
    o_dv = o_kr + 2 * DIFF_HEADS * LANES
    o_gate = o_dv + DIFF_HEADS * DIFF_V_DIM
    o_dk = o_kr + DIFF_HEADS * LANES
    w1 = jnp.concatenate([w[:, :o_kv], w[:, o_gate:]], axis=1).astype(BF16)
    wkpet = jnp.pad(w[:, o_kv:o_kr].T, ((MLA_NOPE_DIM, LANES - MLA_QK_DIM), (0, 0))).astype(BF16)
    wdqt, wdkt, wdvt = (w[:, a:b].T.astype(BF16) for a, b in ((o_kr, o_dk), (o_dk, o_dv), (o_dv, o_gate)))
    wuqt = _pad_last(mla_w_uq[layer].reshape(MLA_Q_RANK, MLA_HEADS, MLA_QK_DIM), LANES)
    wuqt = wuqt.reshape(MLA_Q_RANK, MLA_HEADS * LANES).T.astype(BF16)
    wukv = mla_w_ukv[layer].reshape(MLA_KV_RANK, MLA_HEADS, MLA_NOPE_DIM + MLA_V_DIM)
    wuk = _pad_last(wukv[..., :MLA_NOPE_DIM], LANES).reshape(MLA_KV_RANK, MLA_HEADS * LANES).astype(BF16)
    wuvt = wukv[..., MLA_NOPE_DIM:].reshape(MLA_KV_RANK, MLA_HEADS * MLA_V_DIM).T.astype(BF16)

    qt, k, vt, dqt, dk, dvt, gates = _proj_call(
        x, positions.astype(F32).reshape(B, 1, S),
        (_inv_freq_column(MLA_ROPE_DIM), _inv_freq_column(DIFF_ROT_DIM), row(norm_mix_g[layer]), w1,
         row(mla_q_norm_g[layer]), wuqt, row(mla_kv_norm_g[layer]), wuk, wuvt, wkpet, wdqt, wdkt, wdvt,
         row(b_gate[layer])))

    o_mla = _mla_attn_call(qt, k, vt)
    o_diff = _diff_attn_call(row(diff_lambda_q1[layer]), row(diff_lambda_k1[layer]), row(diff_lambda_q2[layer]),
                             row(diff_lambda_k2[layer]), row(diff_subln_g[layer]), dqt, dk, dvt,
                             lambda_init=lambda_init)

    out = _post_call(x.reshape(T, D), o_mla.reshape(T, -1), o_diff.reshape(T, -1), gates.reshape(T, -1),
                     w_branch_mla[layer].astype(BF16), w_branch_diff[layer].astype(BF16), w_out[layer].astype(BF16),
                     row(norm_ffn_g[layer]), w_ffn_gate[layer].astype(BF16), w_ffn_up[layer].astype(BF16),
                     w_ffn_down[layer].astype(BF16), row(norm_final_g))
    return out.reshape(B, S, D)
```

```python
import functools
import math
from typing import Callable, NamedTuple

import jax
import jax.numpy as jnp
from jax import lax
from jax.experimental import pallas as pl
from jax.experimental.pallas import tpu as pltpu

F32 = jnp.float32
BF16 = jnp.bfloat16

ROPE_THETA = 500000.0
NORM_EPS = 1e-6
LANES = 128
LOG2E = math.log2(math.e)

MLA_HEADS = 8
MLA_NOPE_DIM = 64
MLA_ROPE_DIM = 32
MLA_QK_DIM = MLA_NOPE_DIM + MLA_ROPE_DIM
MLA_V_DIM = 64
MLA_Q_RANK = 384
MLA_KV_RANK = 256
DIFF_HEADS = 4
DIFF_HEAD_DIM = 64
DIFF_V_DIM = 2 * DIFF_HEAD_DIM
DIFF_ROT_DIM = DIFF_HEAD_DIM // 4

VMEM_LIMIT_BYTES = 48 * 1024 * 1024
POST_TILE = 256
KEY_BLOCK = 256
RING = 4
SUM_ROWS = 16
QUERY_TILE = RING * KEY_BLOCK

_C_QLAT = 0
_C_KVLAT = _C_QLAT + MLA_Q_RANK
_C_GATE = _C_KVLAT + MLA_KV_RANK


def _rms(x, g):
    return x * lax.rsqrt(jnp.mean(x * x, axis=-1, keepdims=True) + NORM_EPS) * g


def _dot(a, b):
    return jnp.dot(a, b, preferred_element_type=F32)


def _dot_nt(a, b):
    return lax.dot_general(a, b, (((1,), (1,)), ((), ())), preferred_element_type=F32)


def _rope_rows(xt, base, half, cos, sin):
    x1, x2 = xt[base:base + half], xt[base + half:base + 2 * half]
    return [x1 * cos - x2 * sin, x2 * cos + x1 * sin]


def _const_spec(shape):
    return pl.BlockSpec(shape, lambda *_: (0,) * len(shape), pipeline_mode=pl.Buffered(1))


def _params(n_grid_dims):
    return pltpu.CompilerParams(dimension_semantics=("arbitrary",) * n_grid_dims, vmem_limit_bytes=VMEM_LIMIT_BYTES)


def _proj_kernel(x_ref, pos_ref, fm_ref, fd_ref, gmix_ref, w1_ref, gq_ref, wuqt_ref, gkv_ref, wuk_ref, wuvt_ref,
                 wkpet_ref, wdqt_ref, wdkt_ref, wdvt_ref, bg_ref,
                 qt_out, k_out, vt_out, dqt_out, dk_out, dvt_out, gate_out, *, mla_scale, diff_scale):
    x = x_ref[...]
    xn = _rms(x, gmix_ref[...]).astype(BF16)
    pos = pos_ref[...]
    ang = fm_ref[...] * pos
    cos_m, sin_m = jnp.cos(ang), jnp.sin(ang)
    ang = fd_ref[...] * pos
    cos_d, sin_d = jnp.cos(ang), jnp.sin(ang)
    half_m, half_d = MLA_ROPE_DIM // 2, DIFF_ROT_DIM // 2

    q_lat = _dot(xn, w1_ref[:, _C_QLAT:_C_QLAT + MLA_Q_RANK])
    qt = _dot_nt(wuqt_ref[...], _rms(q_lat, gq_ref[...]).astype(BF16))
    for h in range(MLA_HEADS):
        b0 = h * LANES
        slab = jnp.concatenate([qt[b0:b0 + MLA_NOPE_DIM]] + _rope_rows(qt, b0 + MLA_NOPE_DIM, half_m, cos_m, sin_m)
                               + [qt[b0 + MLA_QK_DIM:b0 + LANES]], axis=0)
        qt_out[b0:b0 + LANES, :] = (slab * mla_scale).astype(BF16)

    kv_lat = _dot(xn, w1_ref[:, _C_KVLAT:_C_KVLAT + MLA_KV_RANK])
    kvn = _rms(kv_lat, gkv_ref[...]).astype(BF16)
    k_nope = _dot(kvn, wuk_ref[...])
    kpt = _dot_nt(wkpet_ref[...], xn)
    k_pe = jnp.concatenate([kpt[:MLA_NOPE_DIM]] + _rope_rows(kpt, MLA_NOPE_DIM, half_m, cos_m, sin_m)
                           + [kpt[MLA_QK_DIM:]], axis=0).T
    for h in range(MLA_HEADS):
        sl = slice(h * LANES, (h + 1) * LANES)
        k_out[:, sl] = (k_nope[:, sl] + k_pe).astype(BF16)
    vt_out[...] = _dot_nt(wuvt_ref[...], kvn).astype(BF16)

    def diff_rope(xt, h):
        parts = []
        for c in range(2):
            base = h * LANES + c * DIFF_HEAD_DIM
            parts += _rope_rows(xt, base, half_d, cos_d, sin_d) + [xt[base + DIFF_ROT_DIM:base + DIFF_HEAD_DIM]]
        return jnp.concatenate(parts, axis=0)

    dqt = _dot_nt(wdqt_ref[...], xn)
    dkt = _dot_nt(wdkt_ref[...], xn)
    for h in range(DIFF_HEADS):
        sl = slice(h * LANES, (h + 1) * LANES)
        dqt_out[sl, :] = (diff_rope(dqt, h) * diff_scale).astype(BF16)
        dk_out[:, sl] = diff_rope(dkt, h).T.astype(BF16)
    dvt_out[...] = _dot_nt(wdvt_ref[...], xn).astype(BF16)

    gate_pre = _dot(xn, w1_ref[:, _C_GATE:]) + bg_ref[...]
    gate_out[...] = (1.0 / (1.0 + jnp.exp(-gate_pre))).astype(BF16)


def _proj_call(x, pos, consts):
    B, S, D = x.shape
    tm = KEY_BLOCK
    n_gate = consts[-1].shape[1]
    row = lambda w: pl.BlockSpec((None, tm, w), lambda b, i: (b, i, 0))
    feat = lambda w: pl.BlockSpec((None, w, tm), lambda b, i: (b, 0, i))
    col = lambda w: pl.BlockSpec((None, None, w, tm), lambda b, i: (b, i, 0, 0))
    row_w = lambda w: jax.ShapeDtypeStruct((B, S, w), BF16)
    feat_w = lambda w: jax.ShapeDtypeStruct((B, w, S), BF16)
    col_w = lambda w: jax.ShapeDtypeStruct((B, S // tm, w, tm), BF16)
    wq, wv, wd = MLA_HEADS * LANES, MLA_HEADS * MLA_V_DIM, DIFF_HEADS * LANES
    return pl.pallas_call(
        functools.partial(_proj_kernel, mla_scale=MLA_QK_DIM ** -0.5 * LOG2E, diff_scale=DIFF_HEAD_DIM ** -0.5 * LOG2E),
        grid=(B, S // tm),
        in_specs=[row(D), feat(1)] + [_const_spec(a.shape) for a in consts],
        out_specs=[feat(wq), row(wq), col(wv), feat(wd), row(wd), col(wd), row(n_gate)],
        out_shape=[feat_w(wq), row_w(wq), col_w(wv), feat_w(wd), row_w(wd), col_w(wd), row_w(n_gate)],
        compiler_params=_params(2),
        name="proj",
    )(x, pos, *consts)


class _Stream(NamedTuple):
    qt: jax.Array
    k_blk: Callable
    vt_blk: Callable
    s_refs: tuple
    p_refs: tuple
    acc_ref: object


def _flash_streams(streams, i):
    tk, tq = streams[0].s_refs[0].shape

    def qk(st, j, r, c0=0):
        s = _dot(st.k_blk(j), st.qt[:, c0:])
        st.s_refs[r][:, c0:] = s
        return jnp.max(s, axis=0, keepdims=True)

    def softmax(st, r, col_max, m):
        m_new = jnp.maximum(m, col_max)
        st.p_refs[r][...] = jnp.exp2(st.s_refs[r][...] - m_new).astype(BF16)
        return m_new, jnp.exp2(m - m_new)

    ones = jnp.ones((SUM_ROWS, tk), BF16)

    def pv(st, j, r, c0=0):
        return _dot(jnp.concatenate([st.vt_blk(j), ones], axis=0), st.p_refs[r][:, c0:])

    def start(st):
        st.acc_ref[...] = jnp.zeros(st.acc_ref.shape, F32)
        for p_ref in st.p_refs:
            p_ref[...] = jnp.zeros((tk, tq), BF16)
        one = jnp.ones((1, tq), F32)
        return jnp.full((1, tq), -jnp.inf, F32), (one,) * RING, tuple(qk(st, r, r) for r in range(RING))

    def trip(st, t, carry):
        m, alphas, maxes = carry
        j0 = RING * t
        new_alphas, new_maxes = [], []
        for r in range(RING):
            contrib = pv(st, jnp.maximum(j0 - RING + r, 0), r)
            m, alpha = softmax(st, r, maxes[r], m)
            st.acc_ref[...] = alphas[r] * st.acc_ref[...] + contrib
            new_alphas.append(alpha)
            new_maxes.append(qk(st, j0 + RING + r, r))
        return m, tuple(new_alphas), tuple(new_maxes)

    def finish(st, carry):
        m, alphas, _ = carry
        j0 = RING * i
        for r in range(RING):
            st.acc_ref[...] = alphas[r] * st.acc_ref[...] + pv(st, jnp.maximum(j0 - RING + r, 0), r)
        for d in range(RING):
            c0 = d * tk
            w = tq - c0
            mask = (lax.broadcasted_iota(jnp.int32, (tk, w), 0) <= lax.broadcasted_iota(jnp.int32, (tk, w), 1))
            s = jnp.where(mask, st.s_refs[d][:, c0:], -jnp.inf)
            m_old = m[:, c0:]
            m_new = jnp.maximum(m_old, jnp.max(s, axis=0, keepdims=True))
            st.p_refs[d][:, c0:] = jnp.exp2(s - m_new).astype(BF16)
            alpha = jnp.exp2(m_old - m_new)
            m = m_new if d == 0 else jnp.concatenate([m[:, :c0], m_new], axis=1)
            st.acc_ref[:, c0:] = alpha * st.acc_ref[:, c0:] + pv(st, j0 + d, d, c0)

    carries = lax.fori_loop(0, i, lambda t, cs: tuple(trip(st, t, c) for st, c in zip(streams, cs)),
                            tuple(start(st) for st in streams))
    for st, c in zip(streams, carries):
        finish(st, c)


def _normalised(acc_ref, dv):
    return acc_ref[:dv, :] / acc_ref[dv:dv + 1, :]


def _key_rows(j, tk):
    return pl.ds(pl.multiple_of(j * tk, tk), tk)


def _split_scratch(scratch, n_streams):
    n = n_streams * RING
    s, p = scratch[:n], scratch[n:2 * n]
    return [(s[k * RING:(k + 1) * RING], p[k * RING:(k + 1) * RING]) for k in range(n_streams)], scratch[2 * n:]


def _attn_scratch(tq, tk, dv):
    return ([pltpu.VMEM((tk, tq), F32)] * (2 * RING) + [pltpu.VMEM((tk, tq), BF16)] * (2 * RING)
            + [pltpu.VMEM((dv + SUM_ROWS, tq), F32)] * 2)


def _mla_attn_kernel(qt_ref, k_ref, vt_ref, o_ref, *scratch):
    i = pl.program_id(2)
    bufs, accs = _split_scratch(scratch, 2)
    tk = bufs[0][0][0].shape[0]
    streams = []
    for hh, ((s_refs, p_refs), acc_ref) in enumerate(zip(bufs, accs)):
        sl = slice(hh * LANES, (hh + 1) * LANES)
        rows = slice(hh * MLA_V_DIM, (hh + 1) * MLA_V_DIM)
        streams.append(_Stream(qt_ref[sl, :], lambda j, sl=sl: k_ref[_key_rows(j, tk), sl],
                               lambda j, rows=rows: vt_ref[j, rows, :], s_refs, p_refs, acc_ref))
    _flash_streams(streams, i)
    outs = [_normalised(acc_ref, MLA_V_DIM) for acc_ref in accs]
    o_ref[...] = jnp.concatenate(outs, axis=0).T.astype(o_ref.dtype)


def _mla_attn_call(qt, k, vt):
    B, S, _ = k.shape
    tq, tk = QUERY_TILE, KEY_BLOCK
    return pl.pallas_call(
        _mla_attn_kernel,
        grid=(B, MLA_HEADS // 2, S // tq),
        in_specs=[pl.BlockSpec((None, 2 * LANES, tq), lambda b, h, i: (b, h, i)),
                  pl.BlockSpec((None, S, 2 * LANES), lambda b, h, i: (b, 0, h)),
                  pl.BlockSpec((None, S // tk, LANES, tk), lambda b, h, i: (b, 0, h, 0))],
        out_specs=pl.BlockSpec((None, tq, LANES), lambda b, h, i: (b, i, h)),
        out_shape=jax.ShapeDtypeStruct((B, S, MLA_HEADS * MLA_V_DIM), BF16),
        scratch_shapes=_attn_scratch(tq, tk, MLA_V_DIM),
        compiler_params=_params(3),
        name="mla_attn",
    )(qt, k, vt)


def _diff_attn_kernel(lq1_ref, lk1_ref, lq2_ref, lk2_ref, gsub_ref, qt_ref, k_ref, vt_ref, o_ref, *scratch,
                      lambda_init):
    i = pl.program_id(2)
    bufs, (acc1_ref, acc2_ref) = _split_scratch(scratch, 2)
    tk = bufs[0][0][0].shape[0]
    qt = qt_ref[...]
    feat = lax.broadcasted_iota(jnp.int32, qt.shape, 0)
    zero = jnp.zeros_like(qt)
    k_blk = lambda j: k_ref[_key_rows(j, tk), :]
    vt_blk = lambda j: vt_ref[j]
    _flash_streams([
        _Stream(jnp.where(feat < DIFF_HEAD_DIM, qt, zero), k_blk, vt_blk, *bufs[0], acc1_ref),
        _Stream(jnp.where(feat >= DIFF_HEAD_DIM, qt, zero), k_blk, vt_blk, *bufs[1], acc2_ref)], i)

    lam = (jnp.exp(jnp.sum(lq1_ref[...] * lk1_ref[...], axis=-1, keepdims=True))
           - jnp.exp(jnp.sum(lq2_ref[...] * lk2_ref[...], axis=-1, keepdims=True)) + lambda_init)
    d = _normalised(acc1_ref, DIFF_V_DIM) - lam * _normalised(acc2_ref, DIFF_V_DIM)
    dn = d * lax.rsqrt(jnp.mean(d * d, axis=0, keepdims=True) + NORM_EPS)
    o_ref[...] = (dn.T * gsub_ref[...] * (1.0 - lambda_init)).astype(o_ref.dtype)


def _diff_attn_call(lq1, lk1, lq2, lk2, gsub, qt, k, vt, *, lambda_init):
    B, S, _ = k.shape
    tq, tk = QUERY_TILE, KEY_BLOCK
    tile = pl.BlockSpec((None, tq, LANES), lambda b, h, i: (b, i, h))
    return pl.pallas_call(
        functools.partial(_diff_attn_kernel, lambda_init=lambda_init),
        grid=(B, DIFF_HEADS, S // tq),
        in_specs=[_const_spec(a.shape) for a in (lq1, lk1, lq2, lk2, gsub)] + [
            pl.BlockSpec((None, LANES, tq), lambda b, h, i: (b, h, i)),
            pl.BlockSpec((None, S, LANES), lambda b, h, i: (b, 0, h)),
            pl.BlockSpec((None, S // tk, LANES, tk), lambda b, h, i: (b, 0, h, 0))],
        out_specs=tile,
        out_shape=jax.ShapeDtypeStruct((B, S, DIFF_HEADS * DIFF_V_DIM), BF16),
        scratch_shapes=_attn_scratch(tq, tk, DIFF_V_DIM),
        compiler_params=_params(3),
        name="diff_attn",
    )(lq1, lk1, lq2, lk2, gsub, qt, k, vt)


def _post_kernel(x_ref, om_ref, od_ref, gate_ref, wbm_ref, wbd_ref, wo_ref, gffn_ref, wg_ref, wu_ref, wd_ref,
                 gfin_ref, out_ref):
    d_model = x_ref.shape[1]
    gates = gate_ref[...].astype(F32)
    merged = (gates[:, :d_model] * _dot(om_ref[...], wbm_ref[...])
              + gates[:, d_model:] * _dot(od_ref[...], wbd_ref[...]))
    x = x_ref[...] + _dot(merged.astype(BF16), wo_ref[...])
    hn = _rms(x, gffn_ref[...]).astype(BF16)
    a = _dot(hn, wg_ref[...])
    hid = (a * (1.0 / (1.0 + jnp.exp(-a)))) * _dot(hn, wu_ref[...])
    x = x + _dot(hid.astype(BF16), wd_ref[...])
    out_ref[...] = _rms(x, gfin_ref[...])


def _post_call(x2, om, od, gates, wbm, wbd, wo, gffn, wg, wu, wd, gfin):
    T, D = x2.shape
    tm = POST_TILE
    row = lambda a: pl.BlockSpec((tm, a.shape[1]), lambda i: (i, 0))
    consts = (wbm, wbd, wo, gffn, wg, wu, wd, gfin)
    return pl.pallas_call(
        _post_kernel,
        grid=(T // tm,),
        in_specs=[row(a) for a in (x2, om, od, gates)] + [_const_spec(a.shape) for a in consts],
        out_specs=pl.BlockSpec((tm, D), lambda i: (i, 0)),
        out_shape=jax.ShapeDtypeStruct((T, D), F32),
        compiler_params=_params(1),
        name="post",
    )(x2, om, od, gates, *consts)


def _inv_freq_column(rot_dim):
    half = rot_dim // 2
    return jnp.exp(-math.log(ROPE_THETA) * jnp.arange(half, dtype=F32) * (2.0 / rot_dim)).reshape(half, 1)


def _pad_last(a, width):
    return jnp.pad(a, [(0, 0)] * (a.ndim - 1) + [(0, width - a.shape[-1])])


def kernel(x, positions, norm_mix_g, w_in, b_gate, mla_q_norm_g, mla_w_uq, mla_kv_norm_g, mla_w_ukv, diff_lambda_q1, diff_lambda_k1, diff_lambda_q2, diff_lambda_k2, diff_subln_g, w_branch_mla, w_branch_diff, w_out, norm_ffn_g, w_ffn_gate, w_ffn_up, w_ffn_down, norm_final_g):
    B, S, D = x.shape
    T = B * S
    assert w_in.shape[0] == 1, "single-layer block"
    assert S % QUERY_TILE == 0
    layer = 0
    lambda_init = 0.8 - 0.6 * math.exp(-0.3 * layer)
    row = lambda a: a.reshape(1, -1)

    w = w_in[layer]
    o_kv = MLA_Q_RANK + MLA_KV_RANK
    o_kr = o_kv + MLA_ROPE_DIM
    o_dv = o_kr + 2 * DIFF_HEADS * LANES
    o_gate = o_dv + DIFF_HEADS * DIFF_V_DIM
    o_dk = o_kr + DIFF_HEADS * LANES
    w1 = jnp.concatenate([w[:, :o_kv], w[:, o_gate:]], axis=1).astype(BF16)
    wkpet = jnp.pad(w[:, o_kv:o_kr].T, ((MLA_NOPE_DIM, LANES - MLA_QK_DIM), (0, 0))).astype(BF16)
    wdqt, wdkt, wdvt = (w[:, a:b].T.astype(BF16) for a, b in ((o_kr, o_dk), (o_dk, o_dv), (o_dv, o_gate)))
    wuqt = _pad_last(mla_w_uq[layer].reshape(MLA_Q_RANK, MLA_HEADS, MLA_QK_DIM), LANES)
    wuqt = wuqt.reshape(MLA_Q_RANK, MLA_HEADS * LANES).T.astype(BF16)
    wukv = mla_w_ukv[layer].reshape(MLA_KV_RANK, MLA_HEADS, MLA_NOPE_DIM + MLA_V_DIM)
    wuk = _pad_last(wukv[..., :MLA_NOPE_DIM], LANES).reshape(MLA_KV_RANK, MLA_HEADS * LANES).astype(BF16)
    wuvt = wukv[..., MLA_NOPE_DIM:].reshape(MLA_KV_RANK, MLA_HEADS * MLA_V_DIM).T.astype(BF16)

    qt, k, vt, dqt, dk, dvt, gates = _proj_call(
        x, positions.astype(F32).reshape(B, 1, S),
        (_inv_freq_column(MLA_ROPE_DIM), _inv_freq_column(DIFF_ROT_DIM), row(norm_mix_g[layer]), w1,
         row(mla_q_norm_g[layer]), wuqt, row(mla_kv_norm_g[layer]), wuk, wuvt, wkpet, wdqt, wdkt, wdvt,
         row(b_gate[layer])))

    o_mla = _mla_attn_call(qt, k, vt)
    o_diff = _diff_attn_call(row(diff_lambda_q1[layer]), row(diff_lambda_k1[layer]), row(diff_lambda_q2[layer]),
                             row(diff_lambda_k2[layer]), row(diff_subln_g[layer]), dqt, dk, dvt,
                             lambda_init=lambda_init)

    out = _post_call(x.reshape(T, D), o_mla.reshape(T, -1), o_diff.reshape(T, -1), gates.reshape(T, -1),
                     w_branch_mla[layer].astype(BF16), w_branch_diff[layer].astype(BF16), w_out[layer].astype(BF16),
                     row(norm_ffn_g[layer]), w_ffn_gate[layer].astype(BF16), w_ffn_up[layer].astype(BF16),
                     w_ffn_down[layer].astype(BF16), row(norm_final_g))
    return out.reshape(B, S, D)
```

```python
import functools
import math
from typing import Callable, NamedTuple

import jax
import jax.numpy as jnp
from jax import lax
from jax.experimental import pallas as pl
from jax.experimental.pallas import tpu as pltpu

F32 = jnp.float32
BF16 = jnp.bfloat16

ROPE_THETA = 500000.0
NORM_EPS = 1e-6
LANES = 128
LOG2E = math.log2(math.e)

MLA_HEADS = 8
MLA_NOPE_DIM = 64
MLA_ROPE_DIM = 32
MLA_QK_DIM = MLA_NOPE_DIM + MLA_ROPE_DIM
MLA_V_DIM = 64
MLA_Q_RANK = 384
MLA_KV_RANK = 256
DIFF_HEADS = 4
DIFF_HEAD_DIM = 64
DIFF_V_DIM = 2 * DIFF_HEAD_DIM
DIFF_ROT_DIM = DIFF_HEAD_DIM // 4

VMEM_LIMIT_BYTES = 48 * 1024 * 1024
POST_TILE = 256
KEY_BLOCK = 256
PROJ_BLOCKS = 2
RING = 4
SUM_ROWS = 16
QUERY_TILE = RING * KEY_BLOCK

_C_QLAT = 0
_C_KVLAT = _C_QLAT + MLA_Q_RANK
_C_GATE = _C_KVLAT + MLA_KV_RANK


def _rms(x, g):
    return x * lax.rsqrt(jnp.mean(x * x, axis=-1, keepdims=True) + NORM_EPS) * g


def _dot(a, b):
    return jnp.dot(a, b, preferred_element_type=F32)


def _dot_nt(a, b):
    return lax.dot_general(a, b, (((1,), (1,)), ((), ())), preferred_element_type=F32)


def _rope_rows(xt, base, half, cos, sin):
    x1, x2 = xt[base:base + half], xt[base + half:base + 2 * half]
    return [x1 * cos - x2 * sin, x2 * cos + x1 * sin]


def _const_spec(shape):
    return pl.BlockSpec(shape, lambda *_: (0,) * len(shape), pipeline_mode=pl.Buffered(1))


def _params(n_grid_dims):
    return pltpu.CompilerParams(dimension_semantics=("arbitrary",) * n_grid_dims, vmem_limit_bytes=VMEM_LIMIT_BYTES)


def _proj_kernel(x_ref, pos_ref, fm_ref, fd_ref, gmix_ref, w1_ref, gq_ref, wuqt_ref, gkv_ref, wuk_ref, wuvt_ref,
                 wkpet_ref, wdqt_ref, wdkt_ref, wdvt_ref, bg_ref,
                 qt_out, k_out, vt_out, dqt_out, dk_out, dvt_out, gate_out, *, mla_scale, diff_scale):
    x = x_ref[...]
    xn = _rms(x, gmix_ref[...]).astype(BF16)
    pos = pos_ref[...]
    ang = fm_ref[...] * pos
    cos_m, sin_m = jnp.cos(ang), jnp.sin(ang)
    ang = fd_ref[...] * pos
    cos_d, sin_d = jnp.cos(ang), jnp.sin(ang)
    half_m, half_d = MLA_ROPE_DIM // 2, DIFF_ROT_DIM // 2

    def store_per_key_block(out_ref, xt):
        for kb in range(PROJ_BLOCKS):
            out_ref[kb] = xt[:, kb * KEY_BLOCK:(kb + 1) * KEY_BLOCK].astype(BF16)

    q_lat = _dot(xn, w1_ref[:, _C_QLAT:_C_QLAT + MLA_Q_RANK])
    qt = _dot_nt(wuqt_ref[...], _rms(q_lat, gq_ref[...]).astype(BF16))
    for h in range(MLA_HEADS):
        b0 = h * LANES
        slab = jnp.concatenate([qt[b0:b0 + MLA_NOPE_DIM]] + _rope_rows(qt, b0 + MLA_NOPE_DIM, half_m, cos_m, sin_m)
                               + [qt[b0 + MLA_QK_DIM:b0 + LANES]], axis=0)
        qt_out[b0:b0 + LANES, :] = (slab * mla_scale).astype(BF16)

    kv_lat = _dot(xn, w1_ref[:, _C_KVLAT:_C_KVLAT + MLA_KV_RANK])
    kvn = _rms(kv_lat, gkv_ref[...]).astype(BF16)
    k_nope = _dot(kvn, wuk_ref[...])
    kpt = _dot_nt(wkpet_ref[...], xn)
    k_pe = jnp.concatenate([kpt[:MLA_NOPE_DIM]] + _rope_rows(kpt, MLA_NOPE_DIM, half_m, cos_m, sin_m)
                           + [kpt[MLA_QK_DIM:]], axis=0).T
    for h in range(MLA_HEADS):
        sl = slice(h * LANES, (h + 1) * LANES)
        k_out[:, sl] = (k_nope[:, sl] + k_pe).astype(BF16)
    store_per_key_block(vt_out, _dot_nt(wuvt_ref[...], kvn))

    def diff_rope(xt, h):
        parts = []
        for c in range(2):
            base = h * LANES + c * DIFF_HEAD_DIM
            parts += _rope_rows(xt, base, half_d, cos_d, sin_d) + [xt[base + DIFF_ROT_DIM:base + DIFF_HEAD_DIM]]
        return jnp.concatenate(parts, axis=0)

    dqt = _dot_nt(wdqt_ref[...], xn)
    dkt = _dot_nt(wdkt_ref[...], xn)
    for h in range(DIFF_HEADS):
        sl = slice(h * LANES, (h + 1) * LANES)
        dqt_out[sl, :] = (diff_rope(dqt, h) * diff_scale).astype(BF16)
        dk_out[:, sl] = diff_rope(dkt, h).T.astype(BF16)
    store_per_key_block(dvt_out, _dot_nt(wdvt_ref[...], xn))

    gate_pre = _dot(xn, w1_ref[:, _C_GATE:]) + bg_ref[...]
    gate_out[...] = (1.0 / (1.0 + jnp.exp(-gate_pre))).astype(BF16)


def _proj_call(x, pos, consts):
    B, S, D = x.shape
    tk = KEY_BLOCK
    tm = PROJ_BLOCKS * tk
    n_gate = consts[-1].shape[1]
    row = lambda w: pl.BlockSpec((None, tm, w), lambda b, i: (b, i, 0))
    feat = lambda w: pl.BlockSpec((None, w, tm), lambda b, i: (b, 0, i))
    col = lambda w: pl.BlockSpec((None, PROJ_BLOCKS, w, tk), lambda b, i: (b, i, 0, 0))
    row_w = lambda w: jax.ShapeDtypeStruct((B, S, w), BF16)
    feat_w = lambda w: jax.ShapeDtypeStruct((B, w, S), BF16)
    col_w = lambda w: jax.ShapeDtypeStruct((B, S // tk, w, tk), BF16)
    wq, wv, wd = MLA_HEADS * LANES, MLA_HEADS * MLA_V_DIM, DIFF_HEADS * LANES
    return pl.pallas_call(
        functools.partial(_proj_kernel, mla_scale=MLA_QK_DIM ** -0.5 * LOG2E, diff_scale=DIFF_HEAD_DIM ** -0.5 * LOG2E),
        grid=(B, S // tm),
        in_specs=[row(D), feat(1)] + [_const_spec(a.shape) for a in consts],
        out_specs=[feat(wq), row(wq), col(wv), feat(wd), row(wd), col(wd), row(n_gate)],
        out_shape=[feat_w(wq), row_w(wq), col_w(wv), feat_w(wd), row_w(wd), col_w(wd), row_w(n_gate)],
        compiler_params=_params(2),
        name="proj",
    )(x, pos, *consts)


class _Stream(NamedTuple):
    qt: jax.Array
    k_blk: Callable
    vt_blk: Callable
    s_refs: tuple
    p_refs: tuple
    acc_ref: object


def _flash_streams(streams, i):
    tk, tq = streams[0].s_refs[0].shape

    def qk(st, j, r):
        s = _dot(st.k_blk(j), st.qt)
        st.s_refs[r][...] = s
        return jnp.max(s, axis=0, keepdims=True)

    def softmax(st, r, col_max, m):
        m_new = jnp.maximum(m, col_max)
        st.p_refs[r][...] = jnp.exp2(st.s_refs[r][...] - m_new).astype(BF16)
        return m_new, jnp.exp2(m - m_new)

    ones = jnp.ones((SUM_ROWS, tk), BF16)

    def pv(st, j, r, c0=0):
        return _dot(jnp.concatenate([st.vt_blk(j), ones], axis=0), st.p_refs[r][:, c0:])

    def start(st):
        st.acc_ref[...] = jnp.zeros(st.acc_ref.shape, F32)
        for p_ref in st.p_refs:
            p_ref[...] = jnp.zeros((tk, tq), BF16)
        one = jnp.ones((1, tq), F32)
        return jnp.full((1, tq), -jnp.inf, F32), (one,) * RING, tuple(qk(st, r, r) for r in range(RING))

    def trip(st, t, carry):
        m, alphas, maxes = carry
        j0 = RING * t
        new_alphas, new_maxes = [], []
        for r in range(RING):
            contrib = pv(st, jnp.maximum(j0 - RING + r, 0), r)
            m, alpha = softmax(st, r, maxes[r], m)
            st.acc_ref[...] = alphas[r] * st.acc_ref[...] + contrib
            new_alphas.append(alpha)
            new_maxes.append(qk(st, j0 + RING + r, r))
        return m, tuple(new_alphas), tuple(new_maxes)

    def finish(st, carry):
        m, alphas, _ = carry
        j0 = RING * i
        for r in range(RING):
            st.acc_ref[...] = alphas[r] * st.acc_ref[...] + pv(st, jnp.maximum(j0 - RING + r, 0), r)
        for d in range(RING):
            c0 = d * tk
            w = tq - c0
            mask = (lax.broadcasted_iota(jnp.int32, (tk, w), 0) <= lax.broadcasted_iota(jnp.int32, (tk, w), 1))
            s = jnp.where(mask, st.s_refs[d][:, c0:], -jnp.inf)
            m_old = m[:, c0:]
            m_new = jnp.maximum(m_old, jnp.max(s, axis=0, keepdims=True))
            st.p_refs[d][:, c0:] = jnp.exp2(s - m_new).astype(BF16)
            alpha = jnp.exp2(m_old - m_new)
            m = m_new if d == 0 else jnp.concatenate([m[:, :c0], m_new], axis=1)
            st.acc_ref[:, c0:] = alpha * st.acc_ref[:, c0:] + pv(st, j0 + d, d, c0)

    carries = lax.fori_loop(0, i, lambda t, cs: tuple(trip(st, t, c) for st, c in zip(streams, cs)),
                            tuple(start(st) for st in streams))
    for st, c in zip(streams, carries):
        finish(st, c)


def _normalised(acc_ref, dv):
    return acc_ref[:dv, :] / acc_ref[dv:dv + 1, :]


def _key_rows(j, tk):
    return pl.ds(pl.multiple_of(j * tk, tk), tk)


def _split_scratch(scratch, n_streams):
    n = n_streams * RING
    s, p = scratch[:n], scratch[n:2 * n]
    return [(s[k * RING:(k + 1) * RING], p[k * RING:(k + 1) * RING]) for k in range(n_streams)], scratch[2 * n:]


def _attn_scratch(tq, tk, dv):
    return ([pltpu.VMEM((tk, tq), F32)] * (2 * RING) + [pltpu.VMEM((tk, tq), BF16)] * (2 * RING)
            + [pltpu.VMEM((dv + SUM_ROWS, tq), F32)] * 2)


def _mla_attn_kernel(qt_ref, k0_ref, k1_ref, vt_ref, o_ref, *scratch):
    i = pl.program_id(2)
    bufs, accs = _split_scratch(scratch, 2)
    tk = bufs[0][0][0].shape[0]
    streams = []
    for hh, (k_ref, (s_refs, p_refs), acc_ref) in enumerate(zip((k0_ref, k1_ref), bufs, accs)):
        rows = slice(hh * MLA_V_DIM, (hh + 1) * MLA_V_DIM)
        streams.append(_Stream(qt_ref[hh * LANES:(hh + 1) * LANES, :], lambda j, k_ref=k_ref: k_ref[_key_rows(j, tk), :],
                               lambda j, rows=rows: vt_ref[j, rows, :], s_refs, p_refs, acc_ref))
    _flash_streams(streams, i)
    outs = [_normalised(acc_ref, MLA_V_DIM) for acc_ref in accs]
    o_ref[...] = jnp.concatenate(outs, axis=0).T.astype(o_ref.dtype)


def _mla_attn_call(qt, k, vt):
    B, S, _ = k.shape
    tq, tk = QUERY_TILE, KEY_BLOCK
    head_keys = lambda hh: pl.BlockSpec((None, S, LANES), lambda b, h, i: (b, 0, 2 * h + hh))
    return pl.pallas_call(
        _mla_attn_kernel,
        grid=(B, MLA_HEADS // 2, S // tq),
        in_specs=[pl.BlockSpec((None, 2 * LANES, tq), lambda b, h, i: (b, h, i)),
                  head_keys(0), head_keys(1),
                  pl.BlockSpec((None, S // tk, LANES, tk), lambda b, h, i: (b, 0, h, 0))],
        out_specs=pl.BlockSpec((None, tq, LANES), lambda b, h, i: (b, i, h)),
        out_shape=jax.ShapeDtypeStruct((B, S, MLA_HEADS * MLA_V_DIM), BF16),
        scratch_shapes=_attn_scratch(tq, tk, MLA_V_DIM),
        compiler_params=_params(3),
        name="mla_attn",
    )(qt, k, k, vt)


def _diff_attn_kernel(lq1_ref, lk1_ref, lq2_ref, lk2_ref, gsub_ref, qt_ref, k_ref, vt_ref, o_ref, *scratch,
                      lambda_init):
    i = pl.program_id(2)
    bufs, (acc1_ref, acc2_ref) = _split_scratch(scratch, 2)
    tk = bufs[0][0][0].shape[0]
    qt = qt_ref[...]
    feat = lax.broadcasted_iota(jnp.int32, qt.shape, 0)
    zero = jnp.zeros_like(qt)
    k_blk = lambda j: k_ref[_key_rows(j, tk), :]
    vt_blk = lambda j: vt_ref[j]
    _flash_streams([
        _Stream(jnp.where(feat < DIFF_HEAD_DIM, qt, zero), k_blk, vt_blk, *bufs[0], acc1_ref),
        _Stream(jnp.where(feat >= DIFF_HEAD_DIM, qt, zero), k_blk, vt_blk, *bufs[1], acc2_ref)], i)

    lam = (jnp.exp(jnp.sum(lq1_ref[...] * lk1_ref[...], axis=-1, keepdims=True))
           - jnp.exp(jnp.sum(lq2_ref[...] * lk2_ref[...], axis=-1, keepdims=True)) + lambda_init)
    d = _normalised(acc1_ref, DIFF_V_DIM) - lam * _normalised(acc2_ref, DIFF_V_DIM)
    dn = d * lax.rsqrt(jnp.mean(d * d, axis=0, keepdims=True) + NORM_EPS)
    o_ref[...] = (dn.T * gsub_ref[...] * (1.0 - lambda_init)).astype(o_ref.dtype)


def _diff_attn_call(lq1, lk1, lq2, lk2, gsub, qt, k, vt, *, lambda_init):
    B, S, _ = k.shape
    tq, tk = QUERY_TILE, KEY_BLOCK
    tile = pl.BlockSpec((None, tq, LANES), lambda b, h, i: (b, i, h))
    return pl.pallas_call(
        functools.partial(_diff_attn_kernel, lambda_init=lambda_init),
        grid=(B, DIFF_HEADS, S // tq),
        in_specs=[_const_spec(a.shape) for a in (lq1, lk1, lq2, lk2, gsub)] + [
            pl.BlockSpec((None, LANES, tq), lambda b, h, i: (b, h, i)),
            pl.BlockSpec((None, S, LANES), lambda b, h, i: (b, 0, h)),
            pl.BlockSpec((None, S // tk, LANES, tk), lambda b, h, i: (b, 0, h, 0))],
        out_specs=tile,
        out_shape=jax.ShapeDtypeStruct((B, S, DIFF_HEADS * DIFF_V_DIM), BF16),
        scratch_shapes=_attn_scratch(tq, tk, DIFF_V_DIM),
        compiler_params=_params(3),
        name="diff_attn",
    )(lq1, lk1, lq2, lk2, gsub, qt, k, vt)


def _post_kernel(x_ref, om_ref, od_ref, gate_ref, wbm_ref, wbd_ref, wo_ref, gffn_ref, wg_ref, wu_ref, wd_ref,
                 gfin_ref, out_ref):
    d_model = x_ref.shape[1]
    gates = gate_ref[...].astype(F32)
    merged = (gates[:, :d_model] * _dot(om_ref[...], wbm_ref[...])
              + gates[:, d_model:] * _dot(od_ref[...], wbd_ref[...]))
    x = x_ref[...] + _dot(merged.astype(BF16), wo_ref[...])
    hn = _rms(x, gffn_ref[...]).astype(BF16)
    a = _dot(hn, wg_ref[...])
    hid = (a * (1.0 / (1.0 + jnp.exp(-a)))) * _dot(hn, wu_ref[...])
    x = x + _dot(hid.astype(BF16), wd_ref[...])
    out_ref[...] = _rms(x, gfin_ref[...])


def _post_call(x2, om, od, gates, wbm, wbd, wo, gffn, wg, wu, wd, gfin):
    T, D = x2.shape
    tm = POST_TILE
    row = lambda a: pl.BlockSpec((tm, a.shape[1]), lambda i: (i, 0))
    consts = (wbm, wbd, wo, gffn, wg, wu, wd, gfin)
    return pl.pallas_call(
        _post_kernel,
        grid=(T // tm,),
        in_specs=[row(a) for a in (x2, om, od, gates)] + [_const_spec(a.shape) for a in consts],
        out_specs=pl.BlockSpec((tm, D), lambda i: (i, 0)),
        out_shape=jax.ShapeDtypeStruct((T, D), F32),
        compiler_params=_params(1),
        name="post",
    )(x2, om, od, gates, *consts)


def _inv_freq_column(rot_dim):
    half = rot_dim // 2
    return jnp.exp(-math.log(ROPE_THETA) * jnp.arange(half, dtype=F32) * (2.0 / rot_dim)).reshape(half, 1)


def _pad_last(a, width):
    return jnp.pad(a, [(0, 0)] * (a.ndim - 1) + [(0, width - a.shape[-1])])


def kernel(x, positions, norm_mix_g, w_in, b_gate, mla_q_norm_g, mla_w_uq, mla_kv_norm_g, mla_w_ukv, diff_lambda_q1, diff_lambda_k1, diff_lambda_q2, diff_lambda_k2, diff_subln_g, w_branch_mla, w_branch_diff, w_out, norm_ffn_g, w_ffn_gate, w_ffn_up, w_ffn_down, norm_final_g):
    B, S, D = x.shape
    T = B * S
    assert w_in.shape[0] == 1, "single-layer block"
    assert S % QUERY_TILE == 0 and S % (PROJ_BLOCKS * KEY_BLOCK) == 0
    layer = 0
    lambda_init = 0.8 - 0.6 * math.exp(-0.3 * layer)
    row = lambda a: a.reshape(1, -1)

    w = w_in[layer]
    o_kv = MLA_Q_RANK + MLA_KV_RANK
    o_kr = o_kv + MLA_ROPE_DIM
    o_dv = o_kr + 2 * DIFF_HEADS * LANES
    o_gate = o_dv + DIFF_HEADS * DIFF_V_DIM
    o_dk = o_kr + DIFF_HEADS * LANES
    w1 = jnp.concatenate([w[:, :o_kv], w[:, o_gate:]], axis=1).astype(BF16)
    wkpet = jnp.pad(w[:, o_kv:o_kr].T, ((MLA_NOPE_DIM, LANES - MLA_QK_DIM), (0, 0))).astype(BF16)
    wdqt, wdkt, wdvt = (w[:, a:b].T.astype(BF16) for a, b in ((o_kr, o_dk), (o_dk, o_dv), (o_dv, o_gate)))
    wuqt = _pad_last(mla_w_uq[layer].reshape(MLA_Q_RANK, MLA_HEADS, MLA_QK_DIM), LANES)
    wuqt = wuqt.reshape(MLA_Q_RANK, MLA_HEADS * LANES).T.astype(BF16)
    wukv = mla_w_ukv[layer].reshape(MLA_KV_RANK, MLA_HEADS, MLA_NOPE_DIM + MLA_V_DIM)
    wuk = _pad_last(wukv[..., :MLA_NOPE_DIM], LANES).reshape(MLA_KV_RANK, MLA_HEADS * LANES).astype(BF16)
    wuvt = wukv[..., MLA_NOPE_DIM:].reshape(MLA_KV_RANK, MLA_HEADS * MLA_V_DIM).T.astype(BF16)

    qt, k, vt, dqt, dk, dvt, gates = _proj_call(
        x, positions.astype(F32).reshape(B, 1, S),
        (_inv_freq_column(MLA_ROPE_DIM), _inv_freq_column(DIFF_ROT_DIM), row(norm_mix_g[layer]), w1,
         row(mla_q_norm_g[layer]), wuqt, row(mla_kv_norm_g[layer]), wuk, wuvt, wkpet, wdqt, wdkt, wdvt,
         row(b_gate[layer])))

    o_mla = _mla_attn_call(qt, k, vt)
    o_diff = _diff_attn_call(row(diff_lambda_q1[layer]), row(diff_lambda_k1[layer]), row(diff_lambda_q2[layer]),
                             row(diff_lambda_k2[layer]), row(diff_subln_g[layer]), dqt, dk, dvt,
                             lambda_init=lambda_init)

    out = _post_call(x.reshape(T, D), o_mla.reshape(T, -1), o_diff.reshape(T, -1), gates.reshape(T, -1),
                     w_branch_mla[layer].astype(BF16), w_branch_diff[layer].astype(BF16), w_out[layer].astype(BF16),
                     row(norm_ffn_g[layer]), w_ffn_gate[layer].astype(BF16), w_ffn_up[layer].astype(BF16),
                     w_ffn_down[layer].astype(BF16), row(norm_final_g))
    return out.reshape(B, S, D)
```

```python
import functools
import math
from typing import Callable, NamedTuple

import jax
import jax.numpy as jnp
from jax import lax
from jax.experimental import pallas as pl
from jax.experimental.pallas import tpu as pltpu

F32 = jnp.float32
BF16 = jnp.bfloat16

ROPE_THETA = 500000.0
NORM_EPS = 1e-6
LANES = 128
LOG2E = math.log2(math.e)

MLA_HEADS = 8
MLA_NOPE_DIM = 64
MLA_ROPE_DIM = 32
MLA_QK_DIM = MLA_NOPE_DIM + MLA_ROPE_DIM
MLA_V_DIM = 64
MLA_Q_RANK = 384
MLA_KV_RANK = 256
DIFF_HEADS = 4
DIFF_HEAD_DIM = 64
DIFF_V_DIM = 2 * DIFF_HEAD_DIM
DIFF_ROT_DIM = DIFF_HEAD_DIM // 4

VMEM_LIMIT_BYTES = 48 * 1024 * 1024
POST_TILE = 256
KEY_BLOCK = 512
PROJ_BLOCKS = 1
RING = 2
SUM_ROWS = 16
QUERY_TILE = RING * KEY_BLOCK

_C_QLAT = 0
_C_KVLAT = _C_QLAT + MLA_Q_RANK
_C_GATE = _C_KVLAT + MLA_KV_RANK


def _rms(x, g):
    return x * lax.rsqrt(jnp.mean(x * x, axis=-1, keepdims=True) + NORM_EPS) * g


def _dot(a, b):
    return jnp.dot(a, b, preferred_element_type=F32)


def _dot_nt(a, b):
    return lax.dot_general(a, b, (((1,), (1,)), ((), ())), preferred_element_type=F32)


def _rope_rows(xt, base, half, cos, sin):
    x1, x2 = xt[base:base + half], xt[base + half:base + 2 * half]
    return [x1 * cos - x2 * sin, x2 * cos + x1 * sin]


def _const_spec(shape):
    return pl.BlockSpec(shape, lambda *_: (0,) * len(shape), pipeline_mode=pl.Buffered(1))


def _params(n_grid_dims):
    return pltpu.CompilerParams(dimension_semantics=("arbitrary",) * n_grid_dims, vmem_limit_bytes=VMEM_LIMIT_BYTES)


def _proj_kernel(x_ref, pos_ref, fm_ref, fd_ref, gmix_ref, w1_ref, gq_ref, wuqt_ref, gkv_ref, wuk_ref, wuvt_ref,
                 wkpet_ref, wdqt_ref, wdkt_ref, wdvt_ref, bg_ref,
                 qt_out, k_out, vt_out, dqt_out, dk_out, dvt_out, gate_out, *, mla_scale, diff_scale):
    x = x_ref[...]
    xn = _rms(x, gmix_ref[...]).astype(BF16)
    pos = pos_ref[...]
    ang = fm_ref[...] * pos
    cos_m, sin_m = jnp.cos(ang), jnp.sin(ang)
    ang = fd_ref[...] * pos
    cos_d, sin_d = jnp.cos(ang), jnp.sin(ang)
    half_m, half_d = MLA_ROPE_DIM // 2, DIFF_ROT_DIM // 2

    def store_per_key_block(out_ref, xt):
        for kb in range(PROJ_BLOCKS):
            out_ref[kb] = xt[:, kb * KEY_BLOCK:(kb + 1) * KEY_BLOCK].astype(BF16)

    q_lat = _dot(xn, w1_ref[:, _C_QLAT:_C_QLAT + MLA_Q_RANK])
    qt = _dot_nt(wuqt_ref[...], _rms(q_lat, gq_ref[...]).astype(BF16))
    for h in range(MLA_HEADS):
        b0 = h * LANES
        slab = jnp.concatenate([qt[b0:b0 + MLA_NOPE_DIM]] + _rope_rows(qt, b0 + MLA_NOPE_DIM, half_m, cos_m, sin_m)
                               + [qt[b0 + MLA_QK_DIM:b0 + LANES]], axis=0)
        qt_out[b0:b0 + LANES, :] = (slab * mla_scale).astype(BF16)

    kv_lat = _dot(xn, w1_ref[:, _C_KVLAT:_C_KVLAT + MLA_KV_RANK])
    kvn = _rms(kv_lat, gkv_ref[...]).astype(BF16)
    k_nope = _dot(kvn, wuk_ref[...])
    kpt = _dot_nt(wkpet_ref[...], xn)
    k_pe = jnp.concatenate([kpt[:MLA_NOPE_DIM]] + _rope_rows(kpt, MLA_NOPE_DIM, half_m, cos_m, sin_m)
                           + [kpt[MLA_QK_DIM:]], axis=0).T
    for h in range(MLA_HEADS):
        sl = slice(h * LANES, (h + 1) * LANES)
        k_out[:, sl] = (k_nope[:, sl] + k_pe).astype(BF16)
    store_per_key_block(vt_out, _dot_nt(wuvt_ref[...], kvn))

    def diff_rope(xt, h):
        parts = []
        for c in range(2):
            base = h * LANES + c * DIFF_HEAD_DIM
            parts += _rope_rows(xt, base, half_d, cos_d, sin_d) + [xt[base + DIFF_ROT_DIM:base + DIFF_HEAD_DIM]]
        return jnp.concatenate(parts, axis=0)

    dqt = _dot_nt(wdqt_ref[...], xn)
    dkt = _dot_nt(wdkt_ref[...], xn)
    for h in range(DIFF_HEADS):
        sl = slice(h * LANES, (h + 1) * LANES)
        dqt_out[sl, :] = (diff_rope(dqt, h) * diff_scale).astype(BF16)
        dk_out[:, sl] = diff_rope(dkt, h).T.astype(BF16)
    store_per_key_block(dvt_out, _dot_nt(wdvt_ref[...], xn))

    gate_pre = _dot(xn, w1_ref[:, _C_GATE:]) + bg_ref[...]
    gate_out[...] = (1.0 / (1.0 + jnp.exp(-gate_pre))).astype(BF16)


def _proj_call(x, pos, consts):
    B, S, D = x.shape
    tk = KEY_BLOCK
    tm = PROJ_BLOCKS * tk
    n_gate = consts[-1].shape[1]
    row = lambda w: pl.BlockSpec((None, tm, w), lambda b, i: (b, i, 0))
    feat = lambda w: pl.BlockSpec((None, w, tm), lambda b, i: (b, 0, i))
    col = lambda w: pl.BlockSpec((None, PROJ_BLOCKS, w, tk), lambda b, i: (b, i, 0, 0))
    row_w = lambda w: jax.ShapeDtypeStruct((B, S, w), BF16)
    feat_w = lambda w: jax.ShapeDtypeStruct((B, w, S), BF16)
    col_w = lambda w: jax.ShapeDtypeStruct((B, S // tk, w, tk), BF16)
    wq, wv, wd = MLA_HEADS * LANES, MLA_HEADS * MLA_V_DIM, DIFF_HEADS * LANES
    return pl.pallas_call(
        functools.partial(_proj_kernel, mla_scale=MLA_QK_DIM ** -0.5 * LOG2E, diff_scale=DIFF_HEAD_DIM ** -0.5 * LOG2E),
        grid=(B, S // tm),
        in_specs=[row(D), feat(1)] + [_const_spec(a.shape) for a in consts],
        out_specs=[feat(wq), row(wq), col(wv), feat(wd), row(wd), col(wd), row(n_gate)],
        out_shape=[feat_w(wq), row_w(wq), col_w(wv), feat_w(wd), row_w(wd), col_w(wd), row_w(n_gate)],
        compiler_params=_params(2),
        name="proj",
    )(x, pos, *consts)


class _Stream(NamedTuple):
    qt: jax.Array
    k_blk: Callable
    vt_blk: Callable
    s_refs: tuple
    p_refs: tuple
    acc_ref: object


def _flash_streams(streams, i):
    tk, tq = streams[0].s_refs[0].shape

    def qk(st, j, r):
        s = _dot(st.k_blk(j), st.qt)
        st.s_refs[r][...] = s
        return jnp.max(s, axis=0, keepdims=True)

    def softmax(st, r, col_max, m):
        m_new = jnp.maximum(m, col_max)
        st.p_refs[r][...] = jnp.exp2(st.s_refs[r][...] - m_new).astype(BF16)
        return m_new, jnp.exp2(m - m_new)

    ones = jnp.ones((SUM_ROWS, tk), BF16)

    def pv(st, j, r, c0=0):
        return _dot(jnp.concatenate([st.vt_blk(j), ones], axis=0), st.p_refs[r][:, c0:])

    def start(st):
        st.acc_ref[...] = jnp.zeros(st.acc_ref.shape, F32)
        for p_ref in st.p_refs:
            p_ref[...] = jnp.zeros((tk, tq), BF16)
        one = jnp.ones((1, tq), F32)
        return jnp.full((1, tq), -jnp.inf, F32), (one,) * RING, tuple(qk(st, r, r) for r in range(RING))

    def trip(st, t, carry):
        m, alphas, maxes = carry
        j0 = RING * t
        new_alphas, new_maxes = [], []
        for r in range(RING):
            contrib = pv(st, jnp.maximum(j0 - RING + r, 0), r)
            m, alpha = softmax(st, r, maxes[r], m)
            st.acc_ref[...] = alphas[r] * st.acc_ref[...] + contrib
            new_alphas.append(alpha)
            new_maxes.append(qk(st, j0 + RING + r, r))
        return m, tuple(new_alphas), tuple(new_maxes)

    def finish(st, carry):
        m, alphas, _ = carry
        j0 = RING * i
        for r in range(RING):
            st.acc_ref[...] = alphas[r] * st.acc_ref[...] + pv(st, jnp.maximum(j0 - RING + r, 0), r)
        for d in range(RING):
            c0 = d * tk
            w = tq - c0
            mask = (lax.broadcasted_iota(jnp.int32, (tk, w), 0) <= lax.broadcasted_iota(jnp.int32, (tk, w), 1))
            s = jnp.where(mask, st.s_refs[d][:, c0:], -jnp.inf)
            m_old = m[:, c0:]
            m_new = jnp.maximum(m_old, jnp.max(s, axis=0, keepdims=True))
            st.p_refs[d][:, c0:] = jnp.exp2(s - m_new).astype(BF16)
            alpha = jnp.exp2(m_old - m_new)
            m = m_new if d == 0 else jnp.concatenate([m[:, :c0], m_new], axis=1)
            st.acc_ref[:, c0:] = alpha * st.acc_ref[:, c0:] + pv(st, j0 + d, d, c0)

    carries = lax.fori_loop(0, i, lambda t, cs: tuple(trip(st, t, c) for st, c in zip(streams, cs)),
                            tuple(start(st) for st in streams))
    for st, c in zip(streams, carries):
        finish(st, c)


def _normalised(acc_ref, dv):
    return acc_ref[:dv, :] / acc_ref[dv:dv + 1, :]


def _key_rows(j, tk):
    return pl.ds(pl.multiple_of(j * tk, tk), tk)


def _split_scratch(scratch, n_streams):
    n = n_streams * RING
    s, p = scratch[:n], scratch[n:2 * n]
    return [(s[k * RING:(k + 1) * RING], p[k * RING:(k + 1) * RING]) for k in range(n_streams)], scratch[2 * n:]


def _attn_scratch(tq, tk, dv):
    return ([pltpu.VMEM((tk, tq), F32)] * (2 * RING) + [pltpu.VMEM((tk, tq), BF16)] * (2 * RING)
            + [pltpu.VMEM((dv + SUM_ROWS, tq), F32)] * 2)


def _mla_attn_kernel(qt_ref, k0_ref, k1_ref, vt_ref, o_ref, *scratch):
    i = pl.program_id(2)
    bufs, accs = _split_scratch(scratch, 2)
    tk = bufs[0][0][0].shape[0]
    streams = []
    for hh, (k_ref, (s_refs, p_refs), acc_ref) in enumerate(zip((k0_ref, k1_ref), bufs, accs)):
        rows = slice(hh * MLA_V_DIM, (hh + 1) * MLA_V_DIM)
        streams.append(_Stream(qt_ref[hh * LANES:(hh + 1) * LANES, :], lambda j, k_ref=k_ref: k_ref[_key_rows(j, tk), :],
                               lambda j, rows=rows: vt_ref[j, rows, :], s_refs, p_refs, acc_ref))
    _flash_streams(streams, i)
    outs = [_normalised(acc_ref, MLA_V_DIM) for acc_ref in accs]
    o_ref[...] = jnp.concatenate(outs, axis=0).T.astype(o_ref.dtype)


def _mla_attn_call(qt, k, vt):
    B, S, _ = k.shape
    tq, tk = QUERY_TILE, KEY_BLOCK
    head_keys = lambda hh: pl.BlockSpec((None, S, LANES), lambda b, h, i: (b, 0, 2 * h + hh))
    return pl.pallas_call(
        _mla_attn_kernel,
        grid=(B, MLA_HEADS // 2, S // tq),
        in_specs=[pl.BlockSpec((None, 2 * LANES, tq), lambda b, h, i: (b, h, i)),
                  head_keys(0), head_keys(1),
                  pl.BlockSpec((None, S // tk, LANES, tk), lambda b, h, i: (b, 0, h, 0))],
        out_specs=pl.BlockSpec((None, tq, LANES), lambda b, h, i: (b, i, h)),
        out_shape=jax.ShapeDtypeStruct((B, S, MLA_HEADS * MLA_V_DIM), BF16),
        scratch_shapes=_attn_scratch(tq, tk, MLA_V_DIM),
        compiler_params=_params(3),
        name="mla_attn",
    )(qt, k, k, vt)


def _diff_attn_kernel(lq1_ref, lk1_ref, lq2_ref, lk2_ref, gsub_ref, qt_ref, k_ref, vt_ref, o_ref, *scratch,
                      lambda_init):
    i = pl.program_id(2)
    bufs, (acc1_ref, acc2_ref) = _split_scratch(scratch, 2)
    tk = bufs[0][0][0].shape[0]
    qt = qt_ref[...]
    feat = lax.broadcasted_iota(jnp.int32, qt.shape, 0)
    zero = jnp.zeros_like(qt)
    k_blk = lambda j: k_ref[_key_rows(j, tk), :]
    vt_blk = lambda j: vt_ref[j]
    _flash_streams([
        _Stream(jnp.where(feat < DIFF_HEAD_DIM, qt, zero), k_blk, vt_blk, *bufs[0], acc1_ref),
        _Stream(jnp.where(feat >= DIFF_HEAD_DIM, qt, zero), k_blk, vt_blk, *bufs[1], acc2_ref)], i)

    lam = (jnp.exp(jnp.sum(lq1_ref[...] * lk1_ref[...], axis=-1, keepdims=True))
           - jnp.exp(jnp.sum(lq2_ref[...] * lk2_ref[...], axis=-1, keepdims=True)) + lambda_init)
    d = _normalised(acc1_ref, DIFF_V_DIM) - lam * _normalised(acc2_ref, DIFF_V_DIM)
    dn = d * lax.rsqrt(jnp.mean(d * d, axis=0, keepdims=True) + NORM_EPS)
    o_ref[...] = (dn.T * gsub_ref[...] * (1.0 - lambda_init)).astype(o_ref.dtype)


def _diff_attn_call(lq1, lk1, lq2, lk2, gsub, qt, k, vt, *, lambda_init):
    B, S, _ = k.shape
    tq, tk = QUERY_TILE, KEY_BLOCK
    tile = pl.BlockSpec((None, tq, LANES), lambda b, h, i: (b, i, h))
    return pl.pallas_call(
        functools.partial(_diff_attn_kernel, lambda_init=lambda_init),
        grid=(B, DIFF_HEADS, S // tq),
        in_specs=[_const_spec(a.shape) for a in (lq1, lk1, lq2, lk2, gsub)] + [
            pl.BlockSpec((None, LANES, tq), lambda b, h, i: (b, h, i)),
            pl.BlockSpec((None, S, LANES), lambda b, h, i: (b, 0, h)),
            pl.BlockSpec((None, S // tk, LANES, tk), lambda b, h, i: (b, 0, h, 0))],
        out_specs=tile,
        out_shape=jax.ShapeDtypeStruct((B, S, DIFF_HEADS * DIFF_V_DIM), BF16),
        scratch_shapes=_attn_scratch(tq, tk, DIFF_V_DIM),
        compiler_params=_params(3),
        name="diff_attn",
    )(lq1, lk1, lq2, lk2, gsub, qt, k, vt)


def _post_kernel(x_ref, om_ref, od_ref, gate_ref, wbm_ref, wbd_ref, wo_ref, gffn_ref, wg_ref, wu_ref, wd_ref,
                 gfin_ref, out_ref):
    d_model = x_ref.shape[1]
    gates = gate_ref[...].astype(F32)
    merged = (gates[:, :d_model] * _dot(om_ref[...], wbm_ref[...])
              + gates[:, d_model:] * _dot(od_ref[...], wbd_ref[...]))
    x = x_ref[...] + _dot(merged.astype(BF16), wo_ref[...])
    hn = _rms(x, gffn_ref[...]).astype(BF16)
    a = _dot(hn, wg_ref[...])
    hid = (a * (1.0 / (1.0 + jnp.exp(-a)))) * _dot(hn, wu_ref[...])
    x = x + _dot(hid.astype(BF16), wd_ref[...])
    out_ref[...] = _rms(x, gfin_ref[...])


def _post_call(x2, om, od, gates, wbm, wbd, wo, gffn, wg, wu, wd, gfin):
    T, D = x2.shape
    tm = POST_TILE
    row = lambda a: pl.BlockSpec((tm, a.shape[1]), lambda i: (i, 0))
    consts = (wbm, wbd, wo, gffn, wg, wu, wd, gfin)
    return pl.pallas_call(
        _post_kernel,
        grid=(T // tm,),
        in_specs=[row(a) for a in (x2, om, od, gates)] + [_const_spec(a.shape) for a in consts],
        out_specs=pl.BlockSpec((tm, D), lambda i: (i, 0)),
        out_shape=jax.ShapeDtypeStruct((T, D), F32),
        compiler_params=_params(1),
        name="post",
    )(x2, om, od, gates, *consts)


def _inv_freq_column(rot_dim):
    half = rot_dim // 2
    return jnp.exp(-math.log(ROPE_THETA) * jnp.arange(half, dtype=F32) * (2.0 / rot_dim)).reshape(half, 1)


def _pad_last(a, width):
    return jnp.pad(a, [(0, 0)] * (a.ndim - 1) + [(0, width - a.shape[-1])])


def kernel(x, positions, norm_mix_g, w_in, b_gate, mla_q_norm_g, mla_w_uq, mla_kv_norm_g, mla_w_ukv, diff_lambda_q1, diff_lambda_k1, diff_lambda_q2, diff_lambda_k2, diff_subln_g, w_branch_mla, w_branch_diff, w_out, norm_ffn_g, w_ffn_gate, w_ffn_up, w_ffn_down, norm_final_g):
    B, S, D = x.shape
    T = B * S
    assert w_in.shape[0] == 1, "single-layer block"
    assert S % QUERY_TILE == 0 and S % (PROJ_BLOCKS * KEY_BLOCK) == 0
    layer = 0
    lambda_init = 0.8 - 0.6 * math.exp(-0.3 * layer)
    row = lambda a: a.reshape(1, -1)

    w = w_in[layer]
    o_kv = MLA_Q_RANK + MLA_KV_RANK
    o_kr = o_kv + MLA_ROPE_DIM
    o_dv = o_kr + 2 * DIFF_HEADS * LANES
    o_gate = o_dv + DIFF_HEADS * DIFF_V_DIM
    o_dk = o_kr + DIFF_HEADS * LANES
    w1 = jnp.concatenate([w[:, :o_kv], w[:, o_gate:]], axis=1).astype(BF16)
    wkpet = jnp.pad(w[:, o_kv:o_kr].T, ((MLA_NOPE_DIM, LANES - MLA_QK_DIM), (0, 0))).astype(BF16)
    wdqt, wdkt, wdvt = (w[:, a:b].T.astype(BF16) for a, b in ((o_kr, o_dk), (o_dk, o_dv), (o_dv, o_gate)))
    wuqt = _pad_last(mla_w_uq[layer].reshape(MLA_Q_RANK, MLA_HEADS, MLA_QK_DIM), LANES)
    wuqt = wuqt.reshape(MLA_Q_RANK, MLA_HEADS * LANES).T.astype(BF16)
    wukv = mla_w_ukv[layer].reshape(MLA_KV_RANK, MLA_HEADS, MLA_NOPE_DIM + MLA_V_DIM)
    wuk = _pad_last(wukv[..., :MLA_NOPE_DIM], LANES).reshape(MLA_KV_RANK, MLA_HEADS * LANES).astype(BF16)
    wuvt = wukv[..., MLA_NOPE_DIM:].reshape(MLA_KV_RANK, MLA_HEADS * MLA_V_DIM).T.astype(BF16)

    qt, k, vt, dqt, dk, dvt, gates = _proj_call(
        x, positions.astype(F32).reshape(B, 1, S),
        (_inv_freq_column(MLA_ROPE_DIM), _inv_freq_column(DIFF_ROT_DIM), row(norm_mix_g[layer]), w1,
         row(mla_q_norm_g[layer]), wuqt, row(mla_kv_norm_g[layer]), wuk, wuvt, wkpet, wdqt, wdkt, wdvt,
         row(b_gate[layer])))

    o_mla = _mla_attn_call(qt, k, vt)
    o_diff = _diff_attn_call(row(diff_lambda_q1[layer]), row(diff_lambda_k1[layer]), row(diff_lambda_q2[layer]),
                             row(diff_lambda_k2[layer]), row(diff_subln_g[layer]), dqt, dk, dvt,
                             lambda_init=lambda_init)

    out = _post_call(x.reshape(T, D), o_mla.reshape(T, -1), o_diff.reshape(T, -1), gates.reshape(T, -1),
                     w_branch_mla[layer].astype(BF16), w_branch_diff[layer].astype(BF16), w_out[layer].astype(BF16),
                     row(norm_ffn_g[layer]), w_ffn_gate[layer].astype(BF16), w_ffn_up[layer].astype(BF16),
                     w_ffn_down[layer].astype(BF16), row(norm_final_g))
    return out.reshape(B, S, D)
```

```python
import functools
import math
from typing import Callable, NamedTuple

import jax
import jax.numpy as jnp
from jax import lax
from jax.experimental import pallas as pl
from jax.experimental.pallas import tpu as pltpu

F32 = jnp.float32
BF16 = jnp.bfloat16

ROPE_THETA = 500000.0
NORM_EPS = 1e-6
LANES = 128
LOG2E = math.log2(math.e)

MLA_HEADS = 8
MLA_NOPE_DIM = 64
MLA_ROPE_DIM = 32
MLA_QK_DIM = MLA_NOPE_DIM + MLA_ROPE_DIM
MLA_V_DIM = 64
MLA_Q_RANK = 384
MLA_KV_RANK = 256
DIFF_HEADS = 4
DIFF_HEAD_DIM = 64
DIFF_V_DIM = 2 * DIFF_HEAD_DIM
DIFF_ROT_DIM = DIFF_HEAD_DIM // 4

VMEM_LIMIT_BYTES = 48 * 1024 * 1024
POST_TILE = 256
KEY_BLOCK = 512
PROJ_BLOCKS = 1
RING = 2
F32_TILE_ROWS, BF16_TILE_ROWS = 8, 16
SUM_ROWS = 16
QUERY_TILE = RING * KEY_BLOCK

_C_QLAT = 0
_C_KVLAT = _C_QLAT + MLA_Q_RANK
_C_GATE = _C_KVLAT + MLA_KV_RANK


def _rms(x, g):
    return x * lax.rsqrt(jnp.mean(x * x, axis=-1, keepdims=True) + NORM_EPS) * g


def _dot(a, b):
    return jnp.dot(a, b, preferred_element_type=F32)


def _dot_nt(a, b):
    return lax.dot_general(a, b, (((1,), (1,)), ((), ())), preferred_element_type=F32)


def _rope_rows(xt, base, half, cos, sin):
    x1, x2 = xt[base:base + half], xt[base + half:base + 2 * half]
    return [x1 * cos - x2 * sin, x2 * cos + x1 * sin]


def _const_spec(shape):
    return pl.BlockSpec(shape, lambda *_: (0,) * len(shape), pipeline_mode=pl.Buffered(1))


def _params(n_grid_dims):
    return pltpu.CompilerParams(dimension_semantics=("arbitrary",) * n_grid_dims, vmem_limit_bytes=VMEM_LIMIT_BYTES)


def _proj_kernel(x_ref, pos_ref, fm_ref, fd_ref, gmix_ref, w1_ref, gq_ref, wuqt_ref, gkv_ref, wuk_ref, wuvt_ref,
                 wkpet_ref, wdqt_ref, wdkt_ref, wdvt_ref, bg_ref,
                 qt_out, k_out, vt_out, dqt_out, dk_out, dvt_out, gate_out, *, mla_scale, diff_scale):
    x = x_ref[...]
    xn = _rms(x, gmix_ref[...]).astype(BF16)
    pos = pos_ref[...]
    ang = fm_ref[...] * pos
    cos_m, sin_m = jnp.cos(ang), jnp.sin(ang)
    ang = fd_ref[...] * pos
    cos_d, sin_d = jnp.cos(ang), jnp.sin(ang)
    half_m, half_d = MLA_ROPE_DIM // 2, DIFF_ROT_DIM // 2

    def store_per_key_block(out_ref, xt):
        for kb in range(PROJ_BLOCKS):
            out_ref[kb] = xt[:, kb * KEY_BLOCK:(kb + 1) * KEY_BLOCK].astype(BF16)

    q_lat = _dot(xn, w1_ref[:, _C_QLAT:_C_QLAT + MLA_Q_RANK])
    qt = _dot_nt(wuqt_ref[...], _rms(q_lat, gq_ref[...]).astype(BF16))
    for h in range(MLA_HEADS):
        b0 = h * LANES
        slab = jnp.concatenate([qt[b0:b0 + MLA_NOPE_DIM]] + _rope_rows(qt, b0 + MLA_NOPE_DIM, half_m, cos_m, sin_m)
                               + [qt[b0 + MLA_QK_DIM:b0 + LANES]], axis=0)
        qt_out[b0:b0 + LANES, :] = (slab * mla_scale).astype(BF16)

    kv_lat = _dot(xn, w1_ref[:, _C_KVLAT:_C_KVLAT + MLA_KV_RANK])
    kvn = _rms(kv_lat, gkv_ref[...]).astype(BF16)
    k_nope = _dot(kvn, wuk_ref[...])
    kpt = _dot_nt(wkpet_ref[...], xn)
    k_pe = jnp.concatenate([kpt[:MLA_NOPE_DIM]] + _rope_rows(kpt, MLA_NOPE_DIM, half_m, cos_m, sin_m)
                           + [kpt[MLA_QK_DIM:]], axis=0).T
    for h in range(MLA_HEADS):
        sl = slice(h * LANES, (h + 1) * LANES)
        k_out[:, sl] = (k_nope[:, sl] + k_pe).astype(BF16)
    store_per_key_block(vt_out, _dot_nt(wuvt_ref[...], kvn))

    def diff_rope(xt, h):
        parts = []
        for c in range(2):
            base = h * LANES + c * DIFF_HEAD_DIM
            parts += _rope_rows(xt, base, half_d, cos_d, sin_d) + [xt[base + DIFF_ROT_DIM:base + DIFF_HEAD_DIM]]
        return jnp.concatenate(parts, axis=0)

    dqt = _dot_nt(wdqt_ref[...], xn)
    dkt = _dot_nt(wdkt_ref[...], xn)
    for h in range(DIFF_HEADS):
        sl = slice(h * LANES, (h + 1) * LANES)
        dqt_out[sl, :] = (diff_rope(dqt, h) * diff_scale).astype(BF16)
        dk_out[:, sl] = diff_rope(dkt, h).T.astype(BF16)
    store_per_key_block(dvt_out, _dot_nt(wdvt_ref[...], xn))

    gate_pre = _dot(xn, w1_ref[:, _C_GATE:]) + bg_ref[...]
    gate_out[...] = (1.0 / (1.0 + jnp.exp(-gate_pre))).astype(BF16)


def _proj_call(x, pos, consts):
    B, S, D = x.shape
    tk = KEY_BLOCK
    tm = PROJ_BLOCKS * tk
    n_gate = consts[-1].shape[1]
    row = lambda w: pl.BlockSpec((None, tm, w), lambda b, i: (b, i, 0))
    feat = lambda w: pl.BlockSpec((None, w, tm), lambda b, i: (b, 0, i))
    col = lambda w: pl.BlockSpec((None, PROJ_BLOCKS, w, tk), lambda b, i: (b, i, 0, 0))
    row_w = lambda w: jax.ShapeDtypeStruct((B, S, w), BF16)
    feat_w = lambda w: jax.ShapeDtypeStruct((B, w, S), BF16)
    col_w = lambda w: jax.ShapeDtypeStruct((B, S // tk, w, tk), BF16)
    wq, wv, wd = MLA_HEADS * LANES, MLA_HEADS * MLA_V_DIM, DIFF_HEADS * LANES
    return pl.pallas_call(
        functools.partial(_proj_kernel, mla_scale=MLA_QK_DIM ** -0.5 * LOG2E, diff_scale=DIFF_HEAD_DIM ** -0.5 * LOG2E),
        grid=(B, S // tm),
        in_specs=[row(D), feat(1)] + [_const_spec(a.shape) for a in consts],
        out_specs=[feat(wq), row(wq), col(wv), feat(wd), row(wd), col(wd), row(n_gate)],
        out_shape=[feat_w(wq), row_w(wq), col_w(wv), feat_w(wd), row_w(wd), col_w(wd), row_w(n_gate)],
        compiler_params=_params(2),
        name="proj",
    )(x, pos, *consts)


class _Stream(NamedTuple):
    qt: jax.Array
    k_blk: Callable
    vt_blk: Callable
    s_refs: tuple
    p_refs: tuple
    acc_ref: object


def _flash_streams(streams, i):
    tk, tq = KEY_BLOCK, streams[0].qt.shape[1]
    n_slabs = tq // LANES

    def store_cols(ref, x, c0=0):
        for c in range(c0 // LANES, n_slabs):
            ref[c, :tk, :] = x[:, c * LANES - c0:(c + 1) * LANES - c0]

    def load_cols(ref, c0=0):
        return jnp.concatenate([ref[c, :tk, :] for c in range(c0 // LANES, n_slabs)], axis=1)

    def qk(st, j, r):
        s = _dot(st.k_blk(j), st.qt)
        store_cols(st.s_refs[r], s)
        return jnp.max(s, axis=0, keepdims=True)

    def softmax(st, r, col_max, m):
        m_new = jnp.maximum(m, col_max)
        for c in range(n_slabs):
            cols = slice(c * LANES, (c + 1) * LANES)
            st.p_refs[r][c, :tk, :] = jnp.exp2(st.s_refs[r][c, :tk, :] - m_new[:, cols]).astype(BF16)
        return m_new, jnp.exp2(m - m_new)

    ones = jnp.ones((SUM_ROWS, tk), BF16)

    def pv(st, j, r, c0=0):
        return _dot(jnp.concatenate([st.vt_blk(j), ones], axis=0), load_cols(st.p_refs[r], c0))

    def start(st):
        st.acc_ref[...] = jnp.zeros(st.acc_ref.shape, F32)
        for p_ref in st.p_refs:
            p_ref[...] = jnp.zeros(p_ref.shape, BF16)
        one = jnp.ones((1, tq), F32)
        return jnp.full((1, tq), -jnp.inf, F32), (one,) * RING, tuple(qk(st, r, r) for r in range(RING))

    def trip(st, t, carry):
        m, alphas, maxes = carry
        j0 = RING * t
        new_alphas, new_maxes = [], []
        for r in range(RING):
            contrib = pv(st, jnp.maximum(j0 - RING + r, 0), r)
            m, alpha = softmax(st, r, maxes[r], m)
            st.acc_ref[...] = alphas[r] * st.acc_ref[...] + contrib
            new_alphas.append(alpha)
            new_maxes.append(qk(st, j0 + RING + r, r))
        return m, tuple(new_alphas), tuple(new_maxes)

    def finish(st, carry):
        m, alphas, _ = carry
        j0 = RING * i
        for r in range(RING):
            st.acc_ref[...] = alphas[r] * st.acc_ref[...] + pv(st, jnp.maximum(j0 - RING + r, 0), r)
        for d in range(RING):
            c0 = d * tk
            w = tq - c0
            mask = (lax.broadcasted_iota(jnp.int32, (tk, w), 0) <= lax.broadcasted_iota(jnp.int32, (tk, w), 1))
            s = jnp.where(mask, load_cols(st.s_refs[d], c0), -jnp.inf)
            m_old = m[:, c0:]
            m_new = jnp.maximum(m_old, jnp.max(s, axis=0, keepdims=True))
            store_cols(st.p_refs[d], jnp.exp2(s - m_new).astype(BF16), c0)
            alpha = jnp.exp2(m_old - m_new)
            m = m_new if d == 0 else jnp.concatenate([m[:, :c0], m_new], axis=1)
            st.acc_ref[:, c0:] = alpha * st.acc_ref[:, c0:] + pv(st, j0 + d, d, c0)

    carries = lax.fori_loop(0, i, lambda t, cs: tuple(trip(st, t, c) for st, c in zip(streams, cs)),
                            tuple(start(st) for st in streams))
    for st, c in zip(streams, carries):
        finish(st, c)


def _normalised(acc_ref, dv):
    return acc_ref[:dv, :] / acc_ref[dv:dv + 1, :]


def _key_rows(j, tk):
    return pl.ds(pl.multiple_of(j * tk, tk), tk)


def _split_scratch(scratch, n_streams):
    n = n_streams * RING
    s, p = scratch[:n], scratch[n:2 * n]
    return [(s[k * RING:(k + 1) * RING], p[k * RING:(k + 1) * RING]) for k in range(n_streams)], scratch[2 * n:]


def _attn_scratch(tq, tk, dv):
    slabs = lambda dtype, pad_rows: pltpu.VMEM((tq // LANES, tk + pad_rows, LANES), dtype)
    return ([slabs(F32, F32_TILE_ROWS)] * (2 * RING) + [slabs(BF16, BF16_TILE_ROWS)] * (2 * RING)
            + [pltpu.VMEM((dv + SUM_ROWS, tq), F32)] * 2)


def _mla_attn_kernel(qt_ref, k0_ref, k1_ref, vt_ref, o_ref, *scratch):
    i = pl.program_id(2)
    bufs, accs = _split_scratch(scratch, 2)
    tk = KEY_BLOCK
    streams = []
    for hh, (k_ref, (s_refs, p_refs), acc_ref) in enumerate(zip((k0_ref, k1_ref), bufs, accs)):
        rows = slice(hh * MLA_V_DIM, (hh + 1) * MLA_V_DIM)
        streams.append(_Stream(qt_ref[hh * LANES:(hh + 1) * LANES, :], lambda j, k_ref=k_ref: k_ref[_key_rows(j, tk), :],
                               lambda j, rows=rows: vt_ref[j, rows, :], s_refs, p_refs, acc_ref))
    _flash_streams(streams, i)
    outs = [_normalised(acc_ref, MLA_V_DIM) for acc_ref in accs]
    o_ref[...] = jnp.concatenate(outs, axis=0).T.astype(o_ref.dtype)


def _mla_attn_call(qt, k, vt):
    B, S, _ = k.shape
    tq, tk = QUERY_TILE, KEY_BLOCK
    head_keys = lambda hh: pl.BlockSpec((None, S, LANES), lambda b, h, i: (b, 0, 2 * h + hh))
    return pl.pallas_call(
        _mla_attn_kernel,
        grid=(B, MLA_HEADS // 2, S // tq),
        in_specs=[pl.BlockSpec((None, 2 * LANES, tq), lambda b, h, i: (b, h, i)),
                  head_keys(0), head_keys(1),
                  pl.BlockSpec((None, S // tk, LANES, tk), lambda b, h, i: (b, 0, h, 0))],
        out_specs=pl.BlockSpec((None, tq, LANES), lambda b, h, i: (b, i, h)),
        out_shape=jax.ShapeDtypeStruct((B, S, MLA_HEADS * MLA_V_DIM), BF16),
        scratch_shapes=_attn_scratch(tq, tk, MLA_V_DIM),
        compiler_params=_params(3),
        name="mla_attn",
    )(qt, k, k, vt)


def _diff_attn_kernel(lq1_ref, lk1_ref, lq2_ref, lk2_ref, gsub_ref, qt_ref, k_ref, vt_ref, o_ref, *scratch,
                      lambda_init):
    i = pl.program_id(2)
    bufs, (acc1_ref, acc2_ref) = _split_scratch(scratch, 2)
    tk = KEY_BLOCK
    qt = qt_ref[...]
    feat = lax.broadcasted_iota(jnp.int32, qt.shape, 0)
    zero = jnp.zeros_like(qt)
    k_blk = lambda j: k_ref[_key_rows(j, tk), :]
    vt_blk = lambda j: vt_ref[j]
    _flash_streams([
        _Stream(jnp.where(feat < DIFF_HEAD_DIM, qt, zero), k_blk, vt_blk, *bufs[0], acc1_ref),
        _Stream(jnp.where(feat >= DIFF_HEAD_DIM, qt, zero), k_blk, vt_blk, *bufs[1], acc2_ref)], i)

    lam = (jnp.exp(jnp.sum(lq1_ref[...] * lk1_ref[...], axis=-1, keepdims=True))
           - jnp.exp(jnp.sum(lq2_ref[...] * lk2_ref[...], axis=-1, keepdims=True)) + lambda_init)
    d = _normalised(acc1_ref, DIFF_V_DIM) - lam * _normalised(acc2_ref, DIFF_V_DIM)
    dn = d * lax.rsqrt(jnp.mean(d * d, axis=0, keepdims=True) + NORM_EPS)
    o_ref[...] = (dn.T * gsub_ref[...] * (1.0 - lambda_init)).astype(o_ref.dtype)


def _diff_attn_call(lq1, lk1, lq2, lk2, gsub, qt, k, vt, *, lambda_init):
    B, S, _ = k.shape
    tq, tk = QUERY_TILE, KEY_BLOCK
    tile = pl.BlockSpec((None, tq, LANES), lambda b, h, i: (b, i, h))
    return pl.pallas_call(
        functools.partial(_diff_attn_kernel, lambda_init=lambda_init),
        grid=(B, DIFF_HEADS, S // tq),
        in_specs=[_const_spec(a.shape) for a in (lq1, lk1, lq2, lk2, gsub)] + [
            pl.BlockSpec((None, LANES, tq), lambda b, h, i: (b, h, i)),
            pl.BlockSpec((None, S, LANES), lambda b, h, i: (b, 0, h)),
            pl.BlockSpec((None, S // tk, LANES, tk), lambda b, h, i: (b, 0, h, 0))],
        out_specs=tile,
        out_shape=jax.ShapeDtypeStruct((B, S, DIFF_HEADS * DIFF_V_DIM), BF16),
        scratch_shapes=_attn_scratch(tq, tk, DIFF_V_DIM),
        compiler_params=_params(3),
        name="diff_attn",
    )(lq1, lk1, lq2, lk2, gsub, qt, k, vt)


def _post_kernel(x_ref, om_ref, od_ref, gate_ref, wbm_ref, wbd_ref, wo_ref, gffn_ref, wg_ref, wu_ref, wd_ref,
                 gfin_ref, out_ref):
    d_model = x_ref.shape[1]
    gates = gate_ref[...].astype(F32)
    merged = (gates[:, :d_model] * _dot(om_ref[...], wbm_ref[...])
              + gates[:, d_model:] * _dot(od_ref[...], wbd_ref[...]))
    x = x_ref[...] + _dot(merged.astype(BF16), wo_ref[...])
    hn = _rms(x, gffn_ref[...]).astype(BF16)
    a = _dot(hn, wg_ref[...])
    hid = (a * (1.0 / (1.0 + jnp.exp(-a)))) * _dot(hn, wu_ref[...])
    x = x + _dot(hid.astype(BF16), wd_ref[...])
    out_ref[...] = _rms(x, gfin_ref[...])


def _post_call(x2, om, od, gates, wbm, wbd, wo, gffn, wg, wu, wd, gfin):
    T, D = x2.shape
    tm = POST_TILE
    row = lambda a: pl.BlockSpec((tm, a.shape[1]), lambda i: (i, 0))
    consts = (wbm, wbd, wo, gffn, wg, wu, wd, gfin)
    return pl.pallas_call(
        _post_kernel,
        grid=(T // tm,),
        in_specs=[row(a) for a in (x2, om, od, gates)] + [_const_spec(a.shape) for a in consts],
        out_specs=pl.BlockSpec((tm, D), lambda i: (i, 0)),
        out_shape=jax.ShapeDtypeStruct((T, D), F32),
        compiler_params=_params(1),
        name="post",
    )(x2, om, od, gates, *consts)


def _inv_freq_column(rot_dim):
    half = rot_dim // 2
    return jnp.exp(-math.log(ROPE_THETA) * jnp.arange(half, dtype=F32) * (2.0 / rot_dim)).reshape(half, 1)


def _pad_last(a, width):
    return jnp.pad(a, [(0, 0)] * (a.ndim - 1) + [(0, width - a.shape[-1])])


def kernel(x, positions, norm_mix_g, w_in, b_gate, mla_q_norm_g, mla_w_uq, mla_kv_norm_g, mla_w_ukv, diff_lambda_q1, diff_lambda_k1, diff_lambda_q2, diff_lambda_k2, diff_subln_g, w_branch_mla, w_branch_diff, w_out, norm_ffn_g, w_ffn_gate, w_ffn_up, w_ffn_down, norm_final_g):
    B, S, D = x.shape
    T = B * S
    assert w_in.shape[0] == 1, "single-layer block"
    assert S % QUERY_TILE == 0 and S % (PROJ_BLOCKS * KEY_BLOCK) == 0
    layer = 0
    lambda_init = 0.8 - 0.6 * math.exp(-0.3 * layer)
    row = lambda a: a.reshape(1, -1)

    w = w_in[layer]
    o_kv = MLA_Q_RANK + MLA_KV_RANK
    o_kr = o_kv + MLA_ROPE_DIM
    o_dv = o_kr + 2 * DIFF_HEADS * LANES
    o_gate = o_dv + DIFF_HEADS * DIFF_V_DIM
    o_dk = o_kr + DIFF_HEADS * LANES
    w1 = jnp.concatenate([w[:, :o_kv], w[:, o_gate:]], axis=1).astype(BF16)
    wkpet = jnp.pad(w[:, o_kv:o_kr].T, ((MLA_NOPE_DIM, LANES - MLA_QK_DIM), (0, 0))).astype(BF16)
    wdqt, wdkt, wdvt = (w[:, a:b].T.astype(BF16) for a, b in ((o_kr, o_dk), (o_dk, o_dv), (o_dv, o_gate)))
    wuqt = _pad_last(mla_w_uq[layer].reshape(MLA_Q_RANK, MLA_HEADS, MLA_QK_DIM), LANES)
    wuqt = wuqt.reshape(MLA_Q_RANK, MLA_HEADS * LANES).T.astype(BF16)
    wukv = mla_w_ukv[layer].reshape(MLA_KV_RANK, MLA_HEADS, MLA_NOPE_DIM + MLA_V_DIM)
    wuk = _pad_last(wukv[..., :MLA_NOPE_DIM], LANES).reshape(MLA_KV_RANK, MLA_HEADS * LANES).astype(BF16)
    wuvt = wukv[..., MLA_NOPE_DIM:].reshape(MLA_KV_RANK, MLA_HEADS * MLA_V_DIM).T.astype(BF16)

    qt, k, vt, dqt, dk, dvt, gates = _proj_call(
        x, positions.astype(F32).reshape(B, 1, S),
        (_inv_freq_column(MLA_ROPE_DIM), _inv_freq_column(DIFF_ROT_DIM), row(norm_mix_g[layer]), w1,
         row(mla_q_norm_g[layer]), wuqt, row(mla_kv_norm_g[layer]), wuk, wuvt, wkpet, wdqt, wdkt, wdvt,
         row(b_gate[layer])))

    o_mla = _mla_attn_call(qt, k, vt)
    o_diff = _diff_attn_call(row(diff_lambda_q1[layer]), row(diff_lambda_k1[layer]), row(diff_lambda_q2[layer]),
                             row(diff_lambda_k2[layer]), row(diff_subln_g[layer]), dqt, dk, dvt,
                             lambda_init=lambda_init)

    out = _post_call(x.reshape(T, D), o_mla.reshape(T, -1), o_diff.reshape(T, -1), gates.reshape(T, -1),
                     w_branch_mla[layer].astype(BF16), w_branch_diff[layer].astype(BF16), w_out[layer].astype(BF16),
                     row(norm_ffn_g[layer]), w_ffn_gate[layer].astype(BF16), w_ffn_up[layer].astype(BF16),
                     w_ffn_down[layer].astype(BF16), row(norm_final_g))
    return out.reshape(B, S, D)
```

```python
import functools
import math
from typing import Callable, NamedTuple

import jax
import jax.numpy as jnp
from jax import lax
from jax.experimental import pallas as pl
from jax.experimental.pallas import tpu as pltpu

F32 = jnp.float32
BF16 = jnp.bfloat16

ROPE_THETA = 500000.0
NORM_EPS = 1e-6
LANES = 128
LOG2E = math.log2(math.e)

MLA_HEADS = 8
MLA_NOPE_DIM = 64
MLA_ROPE_DIM = 32
MLA_QK_DIM = MLA_NOPE_DIM + MLA_ROPE_DIM
MLA_V_DIM = 64
MLA_Q_RANK = 384
MLA_KV_RANK = 256
DIFF_HEADS = 4
DIFF_HEAD_DIM = 64
DIFF_V_DIM = 2 * DIFF_HEAD_DIM
DIFF_ROT_DIM = DIFF_HEAD_DIM // 4

VMEM_LIMIT_BYTES = 48 * 1024 * 1024
POST_TILE = 256
PROJ_TILE = 512
QUERY_TILE = 1024
MLA_KEY_BLOCK = 256
DIFF_KEY_BLOCK = 512
F32_TILE_ROWS, BF16_TILE_ROWS = 8, 16
SUM_ROWS = 16

_C_QLAT = 0
_C_KVLAT = _C_QLAT + MLA_Q_RANK
_C_GATE = _C_KVLAT + MLA_KV_RANK


def _rms(x, g):
    return x * lax.rsqrt(jnp.mean(x * x, axis=-1, keepdims=True) + NORM_EPS) * g


def _dot(a, b):
    return jnp.dot(a, b, preferred_element_type=F32)


def _dot_nt(a, b):
    return lax.dot_general(a, b, (((1,), (1,)), ((), ())), preferred_element_type=F32)


def _rope_rows(xt, base, half, cos, sin):
    x1, x2 = xt[base:base + half], xt[base + half:base + 2 * half]
    return [x1 * cos - x2 * sin, x2 * cos + x1 * sin]


def _const_spec(shape):
    return pl.BlockSpec(shape, lambda *_: (0,) * len(shape), pipeline_mode=pl.Buffered(1))


def _params(n_grid_dims):
    return pltpu.CompilerParams(dimension_semantics=("arbitrary",) * n_grid_dims, vmem_limit_bytes=VMEM_LIMIT_BYTES)


def _proj_kernel(x_ref, pos_ref, fm_ref, fd_ref, gmix_ref, w1_ref, gq_ref, wuqt_ref, gkv_ref, wuk_ref, wuvt_ref,
                 wkpet_ref, wdqt_ref, wdkt_ref, wdvt_ref, bg_ref,
                 qt_out, k_out, vt_out, dqt_out, dk_out, dvt_out, gate_out, *, mla_scale, diff_scale):
    x = x_ref[...]
    xn = _rms(x, gmix_ref[...]).astype(BF16)
    pos = pos_ref[...]
    ang = fm_ref[...] * pos
    cos_m, sin_m = jnp.cos(ang), jnp.sin(ang)
    ang = fd_ref[...] * pos
    cos_d, sin_d = jnp.cos(ang), jnp.sin(ang)
    half_m, half_d = MLA_ROPE_DIM // 2, DIFF_ROT_DIM // 2

    def store_per_key_block(out_ref, xt):
        n_blocks, _, tk = out_ref.shape
        for kb in range(n_blocks):
            out_ref[kb] = xt[:, kb * tk:(kb + 1) * tk].astype(BF16)

    q_lat = _dot(xn, w1_ref[:, _C_QLAT:_C_QLAT + MLA_Q_RANK])
    qt = _dot_nt(wuqt_ref[...], _rms(q_lat, gq_ref[...]).astype(BF16))
    for h in range(MLA_HEADS):
        b0 = h * LANES
        slab = jnp.concatenate([qt[b0:b0 + MLA_NOPE_DIM]] + _rope_rows(qt, b0 + MLA_NOPE_DIM, half_m, cos_m, sin_m)
                               + [qt[b0 + MLA_QK_DIM:b0 + LANES]], axis=0)
        qt_out[b0:b0 + LANES, :] = (slab * mla_scale).astype(BF16)

    kv_lat = _dot(xn, w1_ref[:, _C_KVLAT:_C_KVLAT + MLA_KV_RANK])
    kvn = _rms(kv_lat, gkv_ref[...]).astype(BF16)
    k_nope = _dot(kvn, wuk_ref[...])
    kpt = _dot_nt(wkpet_ref[...], xn)
    k_pe = jnp.concatenate([kpt[:MLA_NOPE_DIM]] + _rope_rows(kpt, MLA_NOPE_DIM, half_m, cos_m, sin_m)
                           + [kpt[MLA_QK_DIM:]], axis=0).T
    for h in range(MLA_HEADS):
        sl = slice(h * LANES, (h + 1) * LANES)
        k_out[:, sl] = (k_nope[:, sl] + k_pe).astype(BF16)
    store_per_key_block(vt_out, _dot_nt(wuvt_ref[...], kvn))

    def diff_rope(xt, h):
        parts = []
        for c in range(2):
            base = h * LANES + c * DIFF_HEAD_DIM
            parts += _rope_rows(xt, base, half_d, cos_d, sin_d) + [xt[base + DIFF_ROT_DIM:base + DIFF_HEAD_DIM]]
        return jnp.concatenate(parts, axis=0)

    dqt = _dot_nt(wdqt_ref[...], xn)
    dkt = _dot_nt(wdkt_ref[...], xn)
    for h in range(DIFF_HEADS):
        sl = slice(h * LANES, (h + 1) * LANES)
        dqt_out[sl, :] = (diff_rope(dqt, h) * diff_scale).astype(BF16)
        dk_out[:, sl] = diff_rope(dkt, h).T.astype(BF16)
    store_per_key_block(dvt_out, _dot_nt(wdvt_ref[...], xn))

    gate_pre = _dot(xn, w1_ref[:, _C_GATE:]) + bg_ref[...]
    gate_out[...] = (1.0 / (1.0 + jnp.exp(-gate_pre))).astype(BF16)


def _proj_call(x, pos, consts):
    B, S, D = x.shape
    tm = PROJ_TILE
    n_gate = consts[-1].shape[1]
    row = lambda w: pl.BlockSpec((None, tm, w), lambda b, i: (b, i, 0))
    feat = lambda w: pl.BlockSpec((None, w, tm), lambda b, i: (b, 0, i))
    col = lambda w, tk: pl.BlockSpec((None, tm // tk, w, tk), lambda b, i: (b, i, 0, 0))
    row_w = lambda w: jax.ShapeDtypeStruct((B, S, w), BF16)
    feat_w = lambda w: jax.ShapeDtypeStruct((B, w, S), BF16)
    col_w = lambda w, tk: jax.ShapeDtypeStruct((B, S // tk, w, tk), BF16)
    wq, wv, wd = MLA_HEADS * LANES, MLA_HEADS * MLA_V_DIM, DIFF_HEADS * LANES
    return pl.pallas_call(
        functools.partial(_proj_kernel, mla_scale=MLA_QK_DIM ** -0.5 * LOG2E, diff_scale=DIFF_HEAD_DIM ** -0.5 * LOG2E),
        grid=(B, S // tm),
        in_specs=[row(D), feat(1)] + [_const_spec(a.shape) for a in consts],
        out_specs=[feat(wq), row(wq), col(wv, MLA_KEY_BLOCK), feat(wd), row(wd), col(wd, DIFF_KEY_BLOCK), row(n_gate)],
        out_shape=[feat_w(wq), row_w(wq), col_w(wv, MLA_KEY_BLOCK), feat_w(wd), row_w(wd), col_w(wd, DIFF_KEY_BLOCK),
                   row_w(n_gate)],
        compiler_params=_params(2),
        name="proj",
    )(x, pos, *consts)


class _Stream(NamedTuple):
    qt: jax.Array
    k_blk: Callable
    vt_blk: Callable
    s_refs: tuple
    p_refs: tuple
    acc_ref: object


def _flash_streams(streams, i, tk):
    tq = streams[0].qt.shape[1]
    ring = tq // tk
    n_slabs = tq // LANES

    def store_cols(ref, x, c0=0):
        for c in range(c0 // LANES, n_slabs):
            ref[c, :tk, :] = x[:, c * LANES - c0:(c + 1) * LANES - c0]

    def load_cols(ref, c0=0):
        return jnp.concatenate([ref[c, :tk, :] for c in range(c0 // LANES, n_slabs)], axis=1)

    def qk(st, j, r):
        s = _dot(st.k_blk(j), st.qt)
        store_cols(st.s_refs[r], s)
        return jnp.max(s, axis=0, keepdims=True)

    def softmax(st, r, col_max, m):
        m_new = jnp.maximum(m, col_max)
        for c in range(n_slabs):
            cols = slice(c * LANES, (c + 1) * LANES)
            st.p_refs[r][c, :tk, :] = jnp.exp2(st.s_refs[r][c, :tk, :] - m_new[:, cols]).astype(BF16)
        return m_new, jnp.exp2(m - m_new)

    ones = jnp.ones((SUM_ROWS, tk), BF16)

    def pv(st, j, r, c0=0):
        return _dot(jnp.concatenate([st.vt_blk(j), ones], axis=0), load_cols(st.p_refs[r], c0))

    def start(st):
        st.acc_ref[...] = jnp.zeros(st.acc_ref.shape, F32)
        for p_ref in st.p_refs:
            p_ref[...] = jnp.zeros(p_ref.shape, BF16)
        one = jnp.ones((1, tq), F32)
        return jnp.full((1, tq), -jnp.inf, F32), (one,) * ring, tuple(qk(st, r, r) for r in range(ring))

    def trip(st, t, carry):
        m, alphas, maxes = carry
        j0 = ring * t
        new_alphas, new_maxes = [], []
        for r in range(ring):
            contrib = pv(st, jnp.maximum(j0 - ring + r, 0), r)
            m, alpha = softmax(st, r, maxes[r], m)
            st.acc_ref[...] = alphas[r] * st.acc_ref[...] + contrib
            new_alphas.append(alpha)
            new_maxes.append(qk(st, j0 + ring + r, r))
        return m, tuple(new_alphas), tuple(new_maxes)

    def finish(st, carry):
        m, alphas, _ = carry
        j0 = ring * i
        for r in range(ring):
            st.acc_ref[...] = alphas[r] * st.acc_ref[...] + pv(st, jnp.maximum(j0 - ring + r, 0), r)
        for d in range(ring):
            c0 = d * tk
            w = tq - c0
            mask = (lax.broadcasted_iota(jnp.int32, (tk, w), 0) <= lax.broadcasted_iota(jnp.int32, (tk, w), 1))
            s = jnp.where(mask, load_cols(st.s_refs[d], c0), -jnp.inf)
            m_old = m[:, c0:]
            m_new = jnp.maximum(m_old, jnp.max(s, axis=0, keepdims=True))
            store_cols(st.p_refs[d], jnp.exp2(s - m_new).astype(BF16), c0)
            alpha = jnp.exp2(m_old - m_new)
            m = m_new if d == 0 else jnp.concatenate([m[:, :c0], m_new], axis=1)
            st.acc_ref[:, c0:] = alpha * st.acc_ref[:, c0:] + pv(st, j0 + d, d, c0)

    carries = lax.fori_loop(0, i, lambda t, cs: tuple(trip(st, t, c) for st, c in zip(streams, cs)),
                            tuple(start(st) for st in streams))
    for st, c in zip(streams, carries):
        finish(st, c)


def _normalised(acc_ref, dv):
    return acc_ref[:dv, :] / acc_ref[dv:dv + 1, :]


def _key_rows(j, tk):
    return pl.ds(pl.multiple_of(j * tk, tk), tk)


def _split_scratch(scratch, n_streams, ring):
    n = n_streams * ring
    s, p = scratch[:n], scratch[n:2 * n]
    return [(s[k * ring:(k + 1) * ring], p[k * ring:(k + 1) * ring]) for k in range(n_streams)], scratch[2 * n:]


def _attn_scratch(tq, tk, dv):
    ring = tq // tk
    slabs = lambda dtype, pad_rows: pltpu.VMEM((tq // LANES, tk + pad_rows, LANES), dtype)
    return ([slabs(F32, F32_TILE_ROWS)] * (2 * ring) + [slabs(BF16, BF16_TILE_ROWS)] * (2 * ring)
            + [pltpu.VMEM((dv + SUM_ROWS, tq), F32)] * 2)


def _mla_attn_kernel(qt_ref, k0_ref, k1_ref, vt_ref, o_ref, *scratch):
    i = pl.program_id(2)
    tk = MLA_KEY_BLOCK
    bufs, accs = _split_scratch(scratch, 2, QUERY_TILE // tk)
    streams = []
    for hh, (k_ref, (s_refs, p_refs), acc_ref) in enumerate(zip((k0_ref, k1_ref), bufs, accs)):
        rows = slice(hh * MLA_V_DIM, (hh + 1) * MLA_V_DIM)
        streams.append(_Stream(qt_ref[hh * LANES:(hh + 1) * LANES, :], lambda j, k_ref=k_ref: k_ref[_key_rows(j, tk), :],
                               lambda j, rows=rows: vt_ref[j, rows, :], s_refs, p_refs, acc_ref))
    _flash_streams(streams, i, tk)
    outs = [_normalised(acc_ref, MLA_V_DIM) for acc_ref in accs]
    o_ref[...] = jnp.concatenate(outs, axis=0).T.astype(o_ref.dtype)


def _mla_attn_call(qt, k, vt):
    B, S, _ = k.shape
    tq, tk = QUERY_TILE, MLA_KEY_BLOCK
    head_keys = lambda hh: pl.BlockSpec((None, S, LANES), lambda b, h, i: (b, 0, 2 * h + hh))
    return pl.pallas_call(
        _mla_attn_kernel,
        grid=(B, MLA_HEADS // 2, S // tq),
        in_specs=[pl.BlockSpec((None, 2 * LANES, tq), lambda b, h, i: (b, h, i)),
                  head_keys(0), head_keys(1),
                  pl.BlockSpec((None, S // tk, LANES, tk), lambda b, h, i: (b, 0, h, 0))],
        out_specs=pl.BlockSpec((None, tq, LANES), lambda b, h, i: (b, i, h)),
        out_shape=jax.ShapeDtypeStruct((B, S, MLA_HEADS * MLA_V_DIM), BF16),
        scratch_shapes=_attn_scratch(tq, tk, MLA_V_DIM),
        compiler_params=_params(3),
        name="mla_attn",
    )(qt, k, k, vt)


def _diff_attn_kernel(lq1_ref, lk1_ref, lq2_ref, lk2_ref, gsub_ref, qt_ref, k_ref, vt_ref, o_ref, *scratch,
                      lambda_init):
    i = pl.program_id(2)
    tk = DIFF_KEY_BLOCK
    bufs, (acc1_ref, acc2_ref) = _split_scratch(scratch, 2, QUERY_TILE // tk)
    qt = qt_ref[...]
    feat = lax.broadcasted_iota(jnp.int32, qt.shape, 0)
    zero = jnp.zeros_like(qt)
    k_blk = lambda j: k_ref[_key_rows(j, tk), :]
    vt_blk = lambda j: vt_ref[j]
    _flash_streams([
        _Stream(jnp.where(feat < DIFF_HEAD_DIM, qt, zero), k_blk, vt_blk, *bufs[0], acc1_ref),
        _Stream(jnp.where(feat >= DIFF_HEAD_DIM, qt, zero), k_blk, vt_blk, *bufs[1], acc2_ref)], i, tk)

    lam = (jnp.exp(jnp.sum(lq1_ref[...] * lk1_ref[...], axis=-1, keepdims=True))
           - jnp.exp(jnp.sum(lq2_ref[...] * lk2_ref[...], axis=-1, keepdims=True)) + lambda_init)
    d = _normalised(acc1_ref, DIFF_V_DIM) - lam * _normalised(acc2_ref, DIFF_V_DIM)
    dn = d * lax.rsqrt(jnp.mean(d * d, axis=0, keepdims=True) + NORM_EPS)
    o_ref[...] = (dn.T * gsub_ref[...] * (1.0 - lambda_init)).astype(o_ref.dtype)


def _diff_attn_call(lq1, lk1, lq2, lk2, gsub, qt, k, vt, *, lambda_init):
    B, S, _ = k.shape
    tq, tk = QUERY_TILE, DIFF_KEY_BLOCK
    tile = pl.BlockSpec((None, tq, LANES), lambda b, h, i: (b, i, h))
    return pl.pallas_call(
        functools.partial(_diff_attn_kernel, lambda_init=lambda_init),
        grid=(B, DIFF_HEADS, S // tq),
        in_specs=[_const_spec(a.shape) for a in (lq1, lk1, lq2, lk2, gsub)] + [
            pl.BlockSpec((None, LANES, tq), lambda b, h, i: (b, h, i)),
            pl.BlockSpec((None, S, LANES), lambda b, h, i: (b, 0, h)),
            pl.BlockSpec((None, S // tk, LANES, tk), lambda b, h, i: (b, 0, h, 0))],
        out_specs=tile,
        out_shape=jax.ShapeDtypeStruct((B, S, DIFF_HEADS * DIFF_V_DIM), BF16),
        scratch_shapes=_attn_scratch(tq, tk, DIFF_V_DIM),
        compiler_params=_params(3),
        name="diff_attn",
    )(lq1, lk1, lq2, lk2, gsub, qt, k, vt)


def _post_kernel(x_ref, om_ref, od_ref, gate_ref, wbm_ref, wbd_ref, wo_ref, gffn_ref, wg_ref, wu_ref, wd_ref,
                 gfin_ref, out_ref):
    d_model = x_ref.shape[1]
    gates = gate_ref[...].astype(F32)
    merged = (gates[:, :d_model] * _dot(om_ref[...], wbm_ref[...])
              + gates[:, d_model:] * _dot(od_ref[...], wbd_ref[...]))
    x = x_ref[...] + _dot(merged.astype(BF16), wo_ref[...])
    hn = _rms(x, gffn_ref[...]).astype(BF16)
    a = _dot(hn, wg_ref[...])
    hid = (a * (1.0 / (1.0 + jnp.exp(-a)))) * _dot(hn, wu_ref[...])
    x = x + _dot(hid.astype(BF16), wd_ref[...])
    out_ref[...] = _rms(x, gfin_ref[...])


def _post_call(x2, om, od, gates, wbm, wbd, wo, gffn, wg, wu, wd, gfin):
    T, D = x2.shape
    tm = POST_TILE
    row = lambda a: pl.BlockSpec((tm, a.shape[1]), lambda i: (i, 0))
    consts = (wbm, wbd, wo, gffn, wg, wu, wd, gfin)
    return pl.pallas_call(
        _post_kernel,
        grid=(T // tm,),
        in_specs=[row(a) for a in (x2, om, od, gates)] + [_const_spec(a.shape) for a in consts],
        out_specs=pl.BlockSpec((tm, D), lambda i: (i, 0)),
        out_shape=jax.ShapeDtypeStruct((T, D), F32),
        compiler_params=_params(1),
        name="post",
    )(x2, om, od, gates, *consts)


def _inv_freq_column(rot_dim):
    half = rot_dim // 2
    return jnp.exp(-math.log(ROPE_THETA) * jnp.arange(half, dtype=F32) * (2.0 / rot_dim)).reshape(half, 1)


def _pad_last(a, width):
    return jnp.pad(a, [(0, 0)] * (a.ndim - 1) + [(0, width - a.shape[-1])])


def kernel(x, positions, norm_mix_g, w_in, b_gate, mla_q_norm_g, mla_w_uq, mla_kv_norm_g, mla_w_ukv, diff_lambda_q1, diff_lambda_k1, diff_lambda_q2, diff_lambda_k2, diff_subln_g, w_branch_mla, w_branch_diff, w_out, norm_ffn_g, w_ffn_gate, w_ffn_up, w_ffn_down, norm_final_g):
    B, S, D = x.shape
    T = B * S
    assert w_in.shape[0] == 1, "single-layer block"
    assert S % QUERY_TILE == 0 and S % PROJ_TILE == 0
    layer = 0
    lambda_init = 0.8 - 0.6 * math.exp(-0.3 * layer)
    row = lambda a: a.reshape(1, -1)

    w = w_in[layer]
    o_kv = MLA_Q_RANK + MLA_KV_RANK
    o_kr = o_kv + MLA_ROPE_DIM
    o_dv = o_kr + 2 * DIFF_HEADS * LANES
    o_gate = o_dv + DIFF_HEADS * DIFF_V_DIM
    o_dk = o_kr + DIFF_HEADS * LANES
    w1 = jnp.concatenate([w[:, :o_kv], w[:, o_gate:]], axis=1).astype(BF16)
    wkpet = jnp.pad(w[:, o_kv:o_kr].T, ((MLA_NOPE_DIM, LANES - MLA_QK_DIM), (0, 0))).astype(BF16)
    wdqt, wdkt, wdvt = (w[:, a:b].T.astype(BF16) for a, b in ((o_kr, o_dk), (o_dk, o_dv), (o_dv, o_gate)))
    wuqt = _pad_last(mla_w_uq[layer].reshape(MLA_Q_RANK, MLA_HEADS, MLA_QK_DIM), LANES)
    wuqt = wuqt.reshape(MLA_Q_RANK, MLA_HEADS * LANES).T.astype(BF16)
    wukv = mla_w_ukv[layer].reshape(MLA_KV_RANK, MLA_HEADS, MLA_NOPE_DIM + MLA_V_DIM)
    wuk = _pad_last(wukv[..., :MLA_NOPE_DIM], LANES).reshape(MLA_KV_RANK, MLA_HEADS * LANES).astype(BF16)
    wuvt = wukv[..., MLA_NOPE_DIM:].reshape(MLA_KV_RANK, MLA_HEADS * MLA_V_DIM).T.astype(BF16)

    qt, k, vt, dqt, dk, dvt, gates = _proj_call(
        x, positions.astype(F32).reshape(B, 1, S),
        (_inv_freq_column(MLA_ROPE_DIM), _inv_freq_column(DIFF_ROT_DIM), row(norm_mix_g[layer]), w1,
         row(mla_q_norm_g[layer]), wuqt, row(mla_kv_norm_g[layer]), wuk, wuvt, wkpet, wdqt, wdkt, wdvt,
         row(b_gate[layer])))

    o_mla = _mla_attn_call(qt, k, vt)
    o_diff = _diff_attn_call(row(diff_lambda_q1[layer]), row(diff_lambda_k1[layer]), row(diff_lambda_q2[layer]),
                             row(diff_lambda_k2[layer]), row(diff_subln_g[layer]), dqt, dk, dvt,
                             lambda_init=lambda_init)

    out = _post_call(x.reshape(T, D), o_mla.reshape(T, -1), o_diff.reshape(T, -1), gates.reshape(T, -1),
                     w_branch_mla[layer].astype(BF16), w_branch_diff[layer].astype(BF16), w_out[layer].astype(BF16),
                     row(norm_ffn_g[layer]), w_ffn_gate[layer].astype(BF16), w_ffn_up[layer].astype(BF16),
                     w_ffn_down[layer].astype(BF16), row(norm_final_g))
    return out.reshape(B, S, D)
```

```python
import functools
import math
from typing import Callable, NamedTuple

import jax
import jax.numpy as jnp
from jax import lax
from jax.experimental import pallas as pl
from jax.experimental.pallas import tpu as pltpu

F32 = jnp.float32
BF16 = jnp.bfloat16

ROPE_THETA = 500000.0
NORM_EPS = 1e-6
LANES = 128
LOG2E = math.log2(math.e)

MLA_HEADS = 8
MLA_NOPE_DIM = 64
MLA_ROPE_DIM = 32
MLA_QK_DIM = MLA_NOPE_DIM + MLA_ROPE_DIM
MLA_V_DIM = 64
MLA_Q_RANK = 384
MLA_KV_RANK = 256
DIFF_HEADS = 4
DIFF_HEAD_DIM = 64
DIFF_V_DIM = 2 * DIFF_HEAD_DIM
DIFF_ROT_DIM = DIFF_HEAD_DIM // 4

VMEM_LIMIT_BYTES = 48 * 1024 * 1024
POST_VMEM_LIMIT_BYTES = 56 * 1024 * 1024
POST_TILE = 512
PROJ_TILE = 512
QUERY_TILE = 1024
MLA_KEY_BLOCK = 256
DIFF_KEY_BLOCK = 512
SLAB_PAD_ROWS = 8
SUM_ROWS = 16

_C_QLAT = 0
_C_KVLAT = _C_QLAT + MLA_Q_RANK
_C_GATE = _C_KVLAT + MLA_KV_RANK


def _rms(x, g):
    return x * lax.rsqrt(jnp.mean(x * x, axis=-1, keepdims=True) + NORM_EPS) * g


def _dot(a, b):
    return jnp.dot(a, b, preferred_element_type=F32)


def _dot_nt(a, b):
    return lax.dot_general(a, b, (((1,), (1,)), ((), ())), preferred_element_type=F32)


def _rope_rows(xt, base, half, cos, sin):
    x1, x2 = xt[base:base + half], xt[base + half:base + 2 * half]
    return [x1 * cos - x2 * sin, x2 * cos + x1 * sin]


def _const_spec(shape):
    return pl.BlockSpec(shape, lambda *_: (0,) * len(shape), pipeline_mode=pl.Buffered(1))


def _params(n_grid_dims, vmem_limit_bytes=VMEM_LIMIT_BYTES):
    return pltpu.CompilerParams(dimension_semantics=("arbitrary",) * n_grid_dims, vmem_limit_bytes=vmem_limit_bytes)


def _proj_kernel(x_ref, pos_ref, fm_ref, fd_ref, gmix_ref, w1_ref, gq_ref, wuqt_ref, gkv_ref, wuk_ref, wuvt_ref,
                 wkpet_ref, wdqt_ref, wdkt_ref, wdvt_ref, bg_ref,
                 qt_out, k_out, vt_out, dqt_out, dk_out, dvt_out, gate_out, *, mla_scale, diff_scale):
    x = x_ref[...]
    xn = _rms(x, gmix_ref[...]).astype(BF16)
    pos = pos_ref[...]
    ang = fm_ref[...] * pos
    cos_m, sin_m = jnp.cos(ang), jnp.sin(ang)
    ang = fd_ref[...] * pos
    cos_d, sin_d = jnp.cos(ang), jnp.sin(ang)
    half_m, half_d = MLA_ROPE_DIM // 2, DIFF_ROT_DIM // 2

    def store_per_key_block(out_ref, xt):
        n_blocks, _, tk = out_ref.shape
        for kb in range(n_blocks):
            out_ref[kb] = xt[:, kb * tk:(kb + 1) * tk].astype(BF16)

    q_lat = _dot(xn, w1_ref[:, _C_QLAT:_C_QLAT + MLA_Q_RANK])
    qt = _dot_nt(wuqt_ref[...], _rms(q_lat, gq_ref[...]).astype(BF16))
    for h in range(MLA_HEADS):
        b0 = h * LANES
        slab = jnp.concatenate([qt[b0:b0 + MLA_NOPE_DIM]] + _rope_rows(qt, b0 + MLA_NOPE_DIM, half_m, cos_m, sin_m)
                               + [qt[b0 + MLA_QK_DIM:b0 + LANES]], axis=0)
        qt_out[b0:b0 + LANES, :] = (slab * mla_scale).astype(BF16)

    kv_lat = _dot(xn, w1_ref[:, _C_KVLAT:_C_KVLAT + MLA_KV_RANK])
    kvn = _rms(kv_lat, gkv_ref[...]).astype(BF16)
    k_nope = _dot(kvn, wuk_ref[...])
    kpt = _dot_nt(wkpet_ref[...], xn)
    k_pe = jnp.concatenate([kpt[:MLA_NOPE_DIM]] + _rope_rows(kpt, MLA_NOPE_DIM, half_m, cos_m, sin_m)
                           + [kpt[MLA_QK_DIM:]], axis=0).T
    for h in range(MLA_HEADS):
        sl = slice(h * LANES, (h + 1) * LANES)
        k_out[:, sl] = (k_nope[:, sl] + k_pe).astype(BF16)
    store_per_key_block(vt_out, _dot_nt(wuvt_ref[...], kvn))

    def diff_rope(xt, h):
        parts = []
        for c in range(2):
            base = h * LANES + c * DIFF_HEAD_DIM
            parts += _rope_rows(xt, base, half_d, cos_d, sin_d) + [xt[base + DIFF_ROT_DIM:base + DIFF_HEAD_DIM]]
        return jnp.concatenate(parts, axis=0)

    dqt = _dot_nt(wdqt_ref[...], xn)
    dkt = _dot_nt(wdkt_ref[...], xn)
    for h in range(DIFF_HEADS):
        sl = slice(h * LANES, (h + 1) * LANES)
        dqt_out[sl, :] = (diff_rope(dqt, h) * diff_scale).astype(BF16)
        dk_out[:, sl] = diff_rope(dkt, h).T.astype(BF16)
    store_per_key_block(dvt_out, _dot_nt(wdvt_ref[...], xn))

    gate_pre = _dot(xn, w1_ref[:, _C_GATE:]) + bg_ref[...]
    gate_out[...] = (1.0 / (1.0 + jnp.exp(-gate_pre))).astype(BF16)


def _proj_call(x, pos, consts):
    B, S, D = x.shape
    tm = PROJ_TILE
    n_gate = consts[-1].shape[1]
    row = lambda w: pl.BlockSpec((None, tm, w), lambda b, i: (b, i, 0))
    feat = lambda w: pl.BlockSpec((None, w, tm), lambda b, i: (b, 0, i))
    col = lambda w, tk: pl.BlockSpec((None, tm // tk, w, tk), lambda b, i: (b, i, 0, 0))
    row_w = lambda w: jax.ShapeDtypeStruct((B, S, w), BF16)
    feat_w = lambda w: jax.ShapeDtypeStruct((B, w, S), BF16)
    col_w = lambda w, tk: jax.ShapeDtypeStruct((B, S // tk, w, tk), BF16)
    wq, wv, wd = MLA_HEADS * LANES, MLA_HEADS * MLA_V_DIM, DIFF_HEADS * LANES
    return pl.pallas_call(
        functools.partial(_proj_kernel, mla_scale=MLA_QK_DIM ** -0.5 * LOG2E, diff_scale=DIFF_HEAD_DIM ** -0.5 * LOG2E),
        grid=(B, S // tm),
        in_specs=[row(D), feat(1)] + [_const_spec(a.shape) for a in consts],
        out_specs=[feat(wq), row(wq), col(wv, MLA_KEY_BLOCK), feat(wd), row(wd), col(wd, DIFF_KEY_BLOCK), row(n_gate)],
        out_shape=[feat_w(wq), row_w(wq), col_w(wv, MLA_KEY_BLOCK), feat_w(wd), row_w(wd), col_w(wd, DIFF_KEY_BLOCK),
                   row_w(n_gate)],
        compiler_params=_params(2),
        name="proj",
    )(x, pos, *consts)


class _Stream(NamedTuple):
    qt: jax.Array
    k_blk: Callable
    vt_blk: Callable
    s_refs: tuple
    acc_ref: object


def _flash_streams(streams, i, tk):
    tq = streams[0].qt.shape[1]
    ring = tq // tk
    n_slabs = tq // LANES

    def store_cols(ref, x):
        for c in range(n_slabs):
            ref[c, :tk, :] = x[:, c * LANES:(c + 1) * LANES]

    def load_cols(ref, c0=0):
        return jnp.concatenate([ref[c, :tk, :] for c in range(c0 // LANES, n_slabs)], axis=1)

    def qk(st, j, r):
        s = _dot(st.k_blk(j), st.qt)
        store_cols(st.s_refs[r], s)
        return jnp.max(s, axis=0, keepdims=True)

    def softmax(st, r, col_max, m):
        m_new = jnp.maximum(m, col_max)
        p = jnp.exp2(load_cols(st.s_refs[r]) - m_new).astype(BF16)
        return m_new, jnp.exp2(m - m_new), p

    ones = jnp.ones((SUM_ROWS, tk), BF16)

    def pv(st, j, p):
        return _dot(jnp.concatenate([st.vt_blk(j), ones], axis=0), p)

    def start(st):
        st.acc_ref[...] = jnp.zeros(st.acc_ref.shape, F32)
        return jnp.full((1, tq), -jnp.inf, F32), tuple(qk(st, r, r) for r in range(ring))

    def trip(st, t, carry):
        m, maxes = carry
        j0 = ring * t
        new_maxes = []
        for r in range(ring):
            m, alpha, p = softmax(st, r, maxes[r], m)
            new_maxes.append(qk(st, j0 + ring + r, r))
            st.acc_ref[...] = alpha * st.acc_ref[...] + pv(st, j0 + r, p)
        return m, tuple(new_maxes)

    def finish(st, carry):
        m, _ = carry
        j0 = ring * i
        for d in range(ring):
            c0 = d * tk
            w = tq - c0
            mask = (lax.broadcasted_iota(jnp.int32, (tk, w), 0) <= lax.broadcasted_iota(jnp.int32, (tk, w), 1))
            s = jnp.where(mask, load_cols(st.s_refs[d], c0), -jnp.inf)
            m_old = m[:, c0:]
            m_new = jnp.maximum(m_old, jnp.max(s, axis=0, keepdims=True))
            p = jnp.exp2(s - m_new).astype(BF16)
            alpha = jnp.exp2(m_old - m_new)
            m = m_new if d == 0 else jnp.concatenate([m[:, :c0], m_new], axis=1)
            st.acc_ref[:, c0:] = alpha * st.acc_ref[:, c0:] + pv(st, j0 + d, p)

    carries = lax.fori_loop(0, i, lambda t, cs: tuple(trip(st, t, c) for st, c in zip(streams, cs)),
                            tuple(start(st) for st in streams))
    for st, c in zip(streams, carries):
        finish(st, c)


def _normalised(acc_ref, dv):
    return acc_ref[:dv, :] / acc_ref[dv:dv + 1, :]


def _key_rows(j, tk):
    return pl.ds(pl.multiple_of(j * tk, tk), tk)


def _split_scratch(scratch, n_streams, ring):
    n = n_streams * ring
    return [scratch[k * ring:(k + 1) * ring] for k in range(n_streams)], scratch[n:]


def _attn_scratch(tq, tk, dv):
    ring = tq // tk
    return ([pltpu.VMEM((tq // LANES, tk + SLAB_PAD_ROWS, LANES), F32)] * (2 * ring)
            + [pltpu.VMEM((dv + SUM_ROWS, tq), F32)] * 2)


def _mla_attn_kernel(qt_ref, k0_ref, k1_ref, vt_ref, o_ref, *scratch):
    i = pl.program_id(2)
    tk = MLA_KEY_BLOCK
    bufs, accs = _split_scratch(scratch, 2, QUERY_TILE // tk)
    streams = []
    for hh, (k_ref, s_refs, acc_ref) in enumerate(zip((k0_ref, k1_ref), bufs, accs)):
        rows = slice(hh * MLA_V_DIM, (hh + 1) * MLA_V_DIM)
        streams.append(_Stream(qt_ref[hh * LANES:(hh + 1) * LANES, :], lambda j, k_ref=k_ref: k_ref[_key_rows(j, tk), :],
                               lambda j, rows=rows: vt_ref[j, rows, :], s_refs, acc_ref))
    _flash_streams(streams, i, tk)
    outs = [_normalised(acc_ref, MLA_V_DIM) for acc_ref in accs]
    o_ref[...] = jnp.concatenate(outs, axis=0).T.astype(o_ref.dtype)


def _mla_attn_call(qt, k, vt):
    B, S, _ = k.shape
    tq, tk = QUERY_TILE, MLA_KEY_BLOCK
    head_keys = lambda hh: pl.BlockSpec((None, S, LANES), lambda b, h, i: (b, 0, 2 * h + hh))
    return pl.pallas_call(
        _mla_attn_kernel,
        grid=(B, MLA_HEADS // 2, S // tq),
        in_specs=[pl.BlockSpec((None, 2 * LANES, tq), lambda b, h, i: (b, h, i)),
                  head_keys(0), head_keys(1),
                  pl.BlockSpec((None, S // tk, LANES, tk), lambda b, h, i: (b, 0, h, 0))],
        out_specs=pl.BlockSpec((None, tq, LANES), lambda b, h, i: (b, i, h)),
        out_shape=jax.ShapeDtypeStruct((B, S, MLA_HEADS * MLA_V_DIM), BF16),
        scratch_shapes=_attn_scratch(tq, tk, MLA_V_DIM),
        compiler_params=_params(3),
        name="mla_attn",
    )(qt, k, k, vt)


def _diff_attn_kernel(lq1_ref, lk1_ref, lq2_ref, lk2_ref, gsub_ref, qt_ref, k_ref, vt_ref, o_ref, *scratch,
                      lambda_init):
    i = pl.program_id(2)
    tk = DIFF_KEY_BLOCK
    bufs, (acc1_ref, acc2_ref) = _split_scratch(scratch, 2, QUERY_TILE // tk)
    qt = qt_ref[...]
    feat = lax.broadcasted_iota(jnp.int32, qt.shape, 0)
    zero = jnp.zeros_like(qt)
    k_blk = lambda j: k_ref[_key_rows(j, tk), :]
    vt_blk = lambda j: vt_ref[j]
    _flash_streams([
        _Stream(jnp.where(feat < DIFF_HEAD_DIM, qt, zero), k_blk, vt_blk, bufs[0], acc1_ref),
        _Stream(jnp.where(feat >= DIFF_HEAD_DIM, qt, zero), k_blk, vt_blk, bufs[1], acc2_ref)], i, tk)

    lam = (jnp.exp(jnp.sum(lq1_ref[...] * lk1_ref[...], axis=-1, keepdims=True))
           - jnp.exp(jnp.sum(lq2_ref[...] * lk2_ref[...], axis=-1, keepdims=True)) + lambda_init)
    d = _normalised(acc1_ref, DIFF_V_DIM) - lam * _normalised(acc2_ref, DIFF_V_DIM)
    dn = d * lax.rsqrt(jnp.mean(d * d, axis=0, keepdims=True) + NORM_EPS)
    o_ref[...] = (dn.T * gsub_ref[...] * (1.0 - lambda_init)).astype(o_ref.dtype)


def _diff_attn_call(lq1, lk1, lq2, lk2, gsub, qt, k, vt, *, lambda_init):
    B, S, _ = k.shape
    tq, tk = QUERY_TILE, DIFF_KEY_BLOCK
    tile = pl.BlockSpec((None, tq, LANES), lambda b, h, i: (b, i, h))
    return pl.pallas_call(
        functools.partial(_diff_attn_kernel, lambda_init=lambda_init),
        grid=(B, DIFF_HEADS, S // tq),
        in_specs=[_const_spec(a.shape) for a in (lq1, lk1, lq2, lk2, gsub)] + [
            pl.BlockSpec((None, LANES, tq), lambda b, h, i: (b, h, i)),
            pl.BlockSpec((None, S, LANES), lambda b, h, i: (b, 0, h)),
            pl.BlockSpec((None, S // tk, LANES, tk), lambda b, h, i: (b, 0, h, 0))],
        out_specs=tile,
        out_shape=jax.ShapeDtypeStruct((B, S, DIFF_HEADS * DIFF_V_DIM), BF16),
        scratch_shapes=_attn_scratch(tq, tk, DIFF_V_DIM),
        compiler_params=_params(3),
        name="diff_attn",
    )(lq1, lk1, lq2, lk2, gsub, qt, k, vt)


def _post_kernel(x_ref, om_ref, od_ref, gate_ref, wbm_ref, wbd_ref, wo_ref, gffn_ref, wg_ref, wu_ref, wd_ref,
                 gfin_ref, out_ref):
    d_model = x_ref.shape[1]
    gates = gate_ref[...].astype(F32)
    merged = (gates[:, :d_model] * _dot(om_ref[...], wbm_ref[...])
              + gates[:, d_model:] * _dot(od_ref[...], wbd_ref[...]))
    x = x_ref[...] + _dot(merged.astype(BF16), wo_ref[...])
    hn = _rms(x, gffn_ref[...]).astype(BF16)
    a = _dot(hn, wg_ref[...])
    hid = (a * (1.0 / (1.0 + jnp.exp(-a)))) * _dot(hn, wu_ref[...])
    x = x + _dot(hid.astype(BF16), wd_ref[...])
    out_ref[...] = _rms(x, gfin_ref[...])


def _post_call(x2, om, od, gates, wbm, wbd, wo, gffn, wg, wu, wd, gfin):
    T, D = x2.shape
    tm = POST_TILE
    row = lambda a: pl.BlockSpec((tm, a.shape[1]), lambda i: (i, 0))
    consts = (wbm, wbd, wo, gffn, wg, wu, wd, gfin)
    return pl.pallas_call(
        _post_kernel,
        grid=(T // tm,),
        in_specs=[row(a) for a in (x2, om, od, gates)] + [_const_spec(a.shape) for a in consts],
        out_specs=pl.BlockSpec((tm, D), lambda i: (i, 0)),
        out_shape=jax.ShapeDtypeStruct((T, D), F32),
        compiler_params=_params(1, POST_VMEM_LIMIT_BYTES),
        name="post",
    )(x2, om, od, gates, *consts)


def _inv_freq_column(rot_dim):
    half = rot_dim // 2
    return jnp.exp(-math.log(ROPE_THETA) * jnp.arange(half, dtype=F32) * (2.0 / rot_dim)).reshape(half, 1)


def _pad_last(a, width):
    return jnp.pad(a, [(0, 0)] * (a.ndim - 1) + [(0, width - a.shape[-1])])


def kernel(x, positions, norm_mix_g, w_in, b_gate, mla_q_norm_g, mla_w_uq, mla_kv_norm_g, mla_w_ukv, diff_lambda_q1, diff_lambda_k1, diff_lambda_q2, diff_lambda_k2, diff_subln_g, w_branch_mla, w_branch_diff, w_out, norm_ffn_g, w_ffn_gate, w_ffn_up, w_ffn_down, norm_final_g):
    B, S, D = x.shape
    T = B * S
    assert w_in.shape[0] == 1, "single-layer block"
    assert S % QUERY_TILE == 0 and S % PROJ_TILE == 0
    layer = 0
    lambda_init = 0.8 - 0.6 * math.exp(-0.3 * layer)
    row = lambda a: a.reshape(1, -1)

    w = w_in[layer]
    o_kv = MLA_Q_RANK + MLA_KV_RANK
    o_kr = o_kv + MLA_ROPE_DIM
    o_dv = o_kr + 2 * DIFF_HEADS * LANES
    o_gate = o_dv + DIFF_HEADS * DIFF_V_DIM
    o_dk = o_kr + DIFF_HEADS * LANES
    w1 = jnp.concatenate([w[:, :o_kv], w[:, o_gate:]], axis=1).astype(BF16)
    wkpet = jnp.pad(w[:, o_kv:o_kr].T, ((MLA_NOPE_DIM, LANES - MLA_QK_DIM), (0, 0))).astype(BF16)
    wdqt, wdkt, wdvt = (w[:, a:b].T.astype(BF16) for a, b in ((o_kr, o_dk), (o_dk, o_dv), (o_dv, o_gate)))
    wuqt = _pad_last(mla_w_uq[layer].reshape(MLA_Q_RANK, MLA_HEADS, MLA_QK_DIM), LANES)
    wuqt = wuqt.reshape(MLA_Q_RANK, MLA_HEADS * LANES).T.astype(BF16)
    wukv = mla_w_ukv[layer].reshape(MLA_KV_RANK, MLA_HEADS, MLA_NOPE_DIM + MLA_V_DIM)
    wuk = _pad_last(wukv[..., :MLA_NOPE_DIM], LANES).reshape(MLA_KV_RANK, MLA_HEADS * LANES).astype(BF16)
    wuvt = wukv[..., MLA_NOPE_DIM:].reshape(MLA_KV_RANK, MLA_HEADS * MLA_V_DIM).T.astype(BF16)

    qt, k, vt, dqt, dk, dvt, gates = _proj_call(
        x, positions.astype(F32).reshape(B, 1, S),
        (_inv_freq_column(MLA_ROPE_DIM), _inv_freq_column(DIFF_ROT_DIM), row(norm_mix_g[layer]), w1,
         row(mla_q_norm_g[layer]), wuqt, row(mla_kv_norm_g[layer]), wuk, wuvt, wkpet, wdqt, wdkt, wdvt,
         row(b_gate[layer])))

    o_mla = _mla_attn_call(qt, k, vt)
    o_diff = _diff_attn_call(row(diff_lambda_q1[layer]), row(diff_lambda_k1[layer]), row(diff_lambda_q2[layer]),
                             row(diff_lambda_k2[layer]), row(diff_subln_g[layer]), dqt, dk, dvt,
                             lambda_init=lambda_init)

    out = _post_call(x.reshape(T, D), o_mla.reshape(T, -1), o_diff.reshape(T, -1), gates.reshape(T, -1),
                     w_branch_mla[layer].astype(BF16), w_branch_diff[layer].astype(BF16), w_out[layer].astype(BF16),
                     row(norm_ffn_g[layer]), w_ffn_gate[layer].astype(BF16), w_ffn_up[layer].astype(BF16),
                     w_ffn_down[layer].astype(BF16), row(norm_final_g))
    return out.reshape(B, S, D)
```

```python
import functools
import math
from typing import Callable, NamedTuple

import jax
import jax.numpy as jnp
from jax import lax
from jax.experimental import pallas as pl
from jax.experimental.pallas import tpu as pltpu

F32 = jnp.float32
BF16 = jnp.bfloat16

ROPE_THETA = 500000.0
NORM_EPS = 1e-6
LANES = 128
LOG2E = math.log2(math.e)

MLA_HEADS = 8
MLA_NOPE_DIM = 64
MLA_ROPE_DIM = 32
MLA_QK_DIM = MLA_NOPE_DIM + MLA_ROPE_DIM
MLA_V_DIM = 64
MLA_Q_RANK = 384
MLA_KV_RANK = 256
DIFF_HEADS = 4
DIFF_HEAD_DIM = 64
DIFF_V_DIM = 2 * DIFF_HEAD_DIM
DIFF_ROT_DIM = DIFF_HEAD_DIM // 4

VMEM_LIMIT_BYTES = 48 * 1024 * 1024
POST_VMEM_LIMIT_BYTES = 56 * 1024 * 1024
POST_TILE = 512
PROJ_TILE = 512
QUERY_TILE = 1024
MLA_KEY_BLOCK = 256
DIFF_KEY_BLOCK = 512
SLAB_PAD_ROWS = 8
SUM_ROWS = 16

_C_QLAT = 0
_C_KVLAT = _C_QLAT + MLA_Q_RANK
_C_GATE = _C_KVLAT + MLA_KV_RANK


def _rms(x, g):
    return x * lax.rsqrt(jnp.mean(x * x, axis=-1, keepdims=True) + NORM_EPS) * g


def _dot(a, b):
    return jnp.dot(a, b, preferred_element_type=F32)


def _dot_nt(a, b):
    return lax.dot_general(a, b, (((1,), (1,)), ((), ())), preferred_element_type=F32)


def _rope_rows(xt, base, half, cos, sin):
    x1, x2 = xt[base:base + half], xt[base + half:base + 2 * half]
    return [x1 * cos - x2 * sin, x2 * cos + x1 * sin]


def _const_spec(shape):
    return pl.BlockSpec(shape, lambda *_: (0,) * len(shape), pipeline_mode=pl.Buffered(1))


def _params(n_grid_dims, vmem_limit_bytes=VMEM_LIMIT_BYTES):
    return pltpu.CompilerParams(dimension_semantics=("arbitrary",) * n_grid_dims, vmem_limit_bytes=vmem_limit_bytes)


def _proj_kernel(x_ref, pos_ref, fm_ref, fd_ref, gmix_ref, w1_ref, gq_ref, wuqt_ref, gkv_ref, wuk_ref, wuvt_ref,
                 wkpet_ref, wdqt_ref, wdkt_ref, wdvt_ref, bg_ref,
                 qt_out, k_out, vt_out, dqt_out, dk_out, dvt_out, gate_out, *, mla_scale, diff_scale):
    x = x_ref[...]
    xn = _rms(x, gmix_ref[...]).astype(BF16)
    pos = pos_ref[...]
    ang = fm_ref[...] * pos
    cos_m, sin_m = jnp.cos(ang), jnp.sin(ang)
    ang = fd_ref[...] * pos
    cos_d, sin_d = jnp.cos(ang), jnp.sin(ang)
    half_m, half_d = MLA_ROPE_DIM // 2, DIFF_ROT_DIM // 2

    def store_per_key_block(out_ref, xt):
        n_blocks, _, tk = out_ref.shape
        for kb in range(n_blocks):
            out_ref[kb] = xt[:, kb * tk:(kb + 1) * tk].astype(BF16)

    q_lat = _dot(xn, w1_ref[:, _C_QLAT:_C_QLAT + MLA_Q_RANK])
    qt = _dot_nt(wuqt_ref[...], _rms(q_lat, gq_ref[...]).astype(BF16))
    for h in range(MLA_HEADS):
        b0 = h * LANES
        slab = jnp.concatenate([qt[b0:b0 + MLA_NOPE_DIM]] + _rope_rows(qt, b0 + MLA_NOPE_DIM, half_m, cos_m, sin_m)
                               + [qt[b0 + MLA_QK_DIM:b0 + LANES]], axis=0)
        qt_out[b0:b0 + LANES, :] = (slab * mla_scale).astype(BF16)

    kv_lat = _dot(xn, w1_ref[:, _C_KVLAT:_C_KVLAT + MLA_KV_RANK])
    kvn = _rms(kv_lat, gkv_ref[...]).astype(BF16)
    k_nope = _dot(kvn, wuk_ref[...])
    kpt = _dot_nt(wkpet_ref[...], xn)
    k_pe = jnp.concatenate([kpt[:MLA_NOPE_DIM]] + _rope_rows(kpt, MLA_NOPE_DIM, half_m, cos_m, sin_m)
                           + [kpt[MLA_QK_DIM:]], axis=0).T
    for h in range(MLA_HEADS):
        sl = slice(h * LANES, (h + 1) * LANES)
        k_out[:, sl] = (k_nope[:, sl] + k_pe).astype(BF16)
    store_per_key_block(vt_out, _dot_nt(wuvt_ref[...], kvn))

    def diff_rope(xt, h):
        parts = []
        for c in range(2):
            base = h * LANES + c * DIFF_HEAD_DIM
            parts += _rope_rows(xt, base, half_d, cos_d, sin_d) + [xt[base + DIFF_ROT_DIM:base + DIFF_HEAD_DIM]]
        return jnp.concatenate(parts, axis=0)

    dqt = _dot_nt(wdqt_ref[...], xn)
    dkt = _dot_nt(wdkt_ref[...], xn)
    for h in range(DIFF_HEADS):
        sl = slice(h * LANES, (h + 1) * LANES)
        dqt_out[sl, :] = (diff_rope(dqt, h) * diff_scale).astype(BF16)
        dk_out[:, sl] = diff_rope(dkt, h).T.astype(BF16)
    store_per_key_block(dvt_out, _dot_nt(wdvt_ref[...], xn))

    gate_pre = _dot(xn, w1_ref[:, _C_GATE:]) + bg_ref[...]
    gate_out[...] = (1.0 / (1.0 + jnp.exp(-gate_pre))).astype(BF16)


def _proj_call(x, pos, consts):
    B, S, D = x.shape
    tm = PROJ_TILE
    n_gate = consts[-1].shape[1]
    row = lambda w: pl.BlockSpec((None, tm, w), lambda b, i: (b, i, 0))
    feat = lambda w: pl.BlockSpec((None, w, tm), lambda b, i: (b, 0, i))
    col = lambda w, tk: pl.BlockSpec((None, tm // tk, w, tk), lambda b, i: (b, i, 0, 0))
    row_w = lambda w: jax.ShapeDtypeStruct((B, S, w), BF16)
    feat_w = lambda w: jax.ShapeDtypeStruct((B, w, S), BF16)
    col_w = lambda w, tk: jax.ShapeDtypeStruct((B, S // tk, w, tk), BF16)
    wq, wv, wd = MLA_HEADS * LANES, MLA_HEADS * MLA_V_DIM, DIFF_HEADS * LANES
    return pl.pallas_call(
        functools.partial(_proj_kernel, mla_scale=MLA_QK_DIM ** -0.5 * LOG2E, diff_scale=DIFF_HEAD_DIM ** -0.5 * LOG2E),
        grid=(B, S // tm),
        in_specs=[row(D), feat(1)] + [_const_spec(a.shape) for a in consts],
        out_specs=[feat(wq), row(wq), col(wv, MLA_KEY_BLOCK), feat(wd), row(wd), col(wd, DIFF_KEY_BLOCK), row(n_gate)],
        out_shape=[feat_w(wq), row_w(wq), col_w(wv, MLA_KEY_BLOCK), feat_w(wd), row_w(wd), col_w(wd, DIFF_KEY_BLOCK),
                   row_w(n_gate)],
        compiler_params=_params(2),
        name="proj",
    )(x, pos, *consts)


class _Stream(NamedTuple):
    qt: jax.Array
    qt_next: jax.Array
    max_ref: object
    k_blk: Callable
    vt_blk: Callable
    s_refs: tuple
    acc_ref: object


def _flash_streams(streams, i, tk):
    tq = streams[0].qt.shape[1]
    ring = tq // tk
    n_slabs = tq // LANES

    def store_cols(ref, x):
        for c in range(n_slabs):
            ref[c, :tk, :] = x[:, c * LANES:(c + 1) * LANES]

    def load_cols(ref, c0=0):
        return jnp.concatenate([ref[c, :tk, :] for c in range(c0 // LANES, n_slabs)], axis=1)

    def qk(st, j, r, qt=None):
        s = _dot(st.k_blk(j), st.qt if qt is None else qt)
        store_cols(st.s_refs[r], s)
        return jnp.max(s, axis=0, keepdims=True)

    def softmax(st, r, col_max, m):
        m_new = jnp.maximum(m, col_max)
        p = jnp.exp2(load_cols(st.s_refs[r]) - m_new).astype(BF16)
        return m_new, jnp.exp2(m - m_new), p

    ones = jnp.ones((SUM_ROWS, tk), BF16)

    def pv(st, j, p):
        return _dot(jnp.concatenate([st.vt_blk(j), ones], axis=0), p)

    def start(st):
        st.acc_ref[...] = jnp.zeros(st.acc_ref.shape, F32)
        return jnp.full((1, tq), -jnp.inf, F32), tuple(st.max_ref[r:r + 1, :] for r in range(ring))

    def trip(st, t, carry):
        m, maxes = carry
        j0 = ring * t
        new_maxes = []
        for r in range(ring):
            m, alpha, p = softmax(st, r, maxes[r], m)
            new_maxes.append(qk(st, j0 + ring + r, r))
            st.acc_ref[...] = alpha * st.acc_ref[...] + pv(st, j0 + r, p)
        return m, tuple(new_maxes)

    def finish(st, carry):
        m, _ = carry
        j0 = ring * i
        for d in range(ring):
            c0 = d * tk
            w = tq - c0
            mask = (lax.broadcasted_iota(jnp.int32, (tk, w), 0) <= lax.broadcasted_iota(jnp.int32, (tk, w), 1))
            s = jnp.where(mask, load_cols(st.s_refs[d], c0), -jnp.inf)
            st.max_ref[d:d + 1, :] = qk(st, d, d, st.qt_next)
            m_old = m[:, c0:]
            m_new = jnp.maximum(m_old, jnp.max(s, axis=0, keepdims=True))
            p = jnp.exp2(s - m_new).astype(BF16)
            alpha = jnp.exp2(m_old - m_new)
            m = m_new if d == 0 else jnp.concatenate([m[:, :c0], m_new], axis=1)
            st.acc_ref[:, c0:] = alpha * st.acc_ref[:, c0:] + pv(st, j0 + d, p)

    @pl.when(i == 0)
    def _():
        for st in streams:
            for r in range(ring):
                st.max_ref[r:r + 1, :] = qk(st, r, r)

    carries = lax.fori_loop(0, i, lambda t, cs: tuple(trip(st, t, c) for st, c in zip(streams, cs)),
                            tuple(start(st) for st in streams))
    for st, c in zip(streams, carries):
        finish(st, c)


def _normalised(acc_ref, dv):
    return acc_ref[:dv, :] / acc_ref[dv:dv + 1, :]


def _key_rows(j, tk):
    return pl.ds(pl.multiple_of(j * tk, tk), tk)


def _split_scratch(scratch, n_streams, ring):
    n = n_streams * ring
    return [scratch[k * ring:(k + 1) * ring] for k in range(n_streams)], scratch[n:]


def _attn_scratch(tq, tk, dv):
    ring = tq // tk
    return ([pltpu.VMEM((tq // LANES, tk + SLAB_PAD_ROWS, LANES), F32)] * (2 * ring)
            + [pltpu.VMEM((dv + SUM_ROWS, tq), F32)] * 2 + [pltpu.VMEM((2 * ring, tq), F32)])


def _mla_attn_kernel(qt_ref, qtn_ref, k0_ref, k1_ref, vt_ref, o_ref, *scratch):
    i = pl.program_id(2)
    tk = MLA_KEY_BLOCK
    ring = QUERY_TILE // tk
    bufs, (*accs, max_ref) = _split_scratch(scratch, 2, ring)
    streams = []
    for hh, (k_ref, s_refs, acc_ref) in enumerate(zip((k0_ref, k1_ref), bufs, accs)):
        rows = slice(hh * MLA_V_DIM, (hh + 1) * MLA_V_DIM)
        sl = slice(hh * LANES, (hh + 1) * LANES)
        streams.append(_Stream(qt_ref[sl, :], qtn_ref[sl, :], max_ref.at[hh * ring:(hh + 1) * ring],
                               lambda j, k_ref=k_ref: k_ref[_key_rows(j, tk), :],
                               lambda j, rows=rows: vt_ref[j, rows, :], s_refs, acc_ref))
    _flash_streams(streams, i, tk)
    outs = [_normalised(acc_ref, MLA_V_DIM) for acc_ref in accs]
    o_ref[...] = jnp.concatenate(outs, axis=0).T.astype(o_ref.dtype)


def _mla_attn_call(qt, k, vt):
    B, S, _ = k.shape
    tq, tk = QUERY_TILE, MLA_KEY_BLOCK
    head_keys = lambda hh: pl.BlockSpec((None, S, LANES), lambda b, h, i: (b, 0, 2 * h + hh))
    return pl.pallas_call(
        _mla_attn_kernel,
        grid=(B, MLA_HEADS // 2, S // tq),
        in_specs=[pl.BlockSpec((None, 2 * LANES, tq), lambda b, h, i: (b, h, i)),
                  pl.BlockSpec((None, 2 * LANES, tq), lambda b, h, i: (b, h, jnp.minimum(i + 1, S // tq - 1))),
                  head_keys(0), head_keys(1),
                  pl.BlockSpec((None, S // tk, LANES, tk), lambda b, h, i: (b, 0, h, 0))],
        out_specs=pl.BlockSpec((None, tq, LANES), lambda b, h, i: (b, i, h)),
        out_shape=jax.ShapeDtypeStruct((B, S, MLA_HEADS * MLA_V_DIM), BF16),
        scratch_shapes=_attn_scratch(tq, tk, MLA_V_DIM),
        compiler_params=_params(3),
        name="mla_attn",
    )(qt, qt, k, k, vt)


def _diff_attn_kernel(lq1_ref, lk1_ref, lq2_ref, lk2_ref, gsub_ref, qt_ref, qtn_ref, k_ref, vt_ref, o_ref, *scratch,
                      lambda_init):
    i = pl.program_id(2)
    tk = DIFF_KEY_BLOCK
    ring = QUERY_TILE // tk
    bufs, (acc1_ref, acc2_ref, max_ref) = _split_scratch(scratch, 2, ring)
    feat = lax.broadcasted_iota(jnp.int32, qt_ref.shape, 0)
    k_blk = lambda j: k_ref[_key_rows(j, tk), :]
    vt_blk = lambda j: vt_ref[j]

    def comp(ref, c):
        qt = ref[...]
        return jnp.where((feat >= DIFF_HEAD_DIM) == bool(c), qt, jnp.zeros_like(qt))

    _flash_streams([_Stream(comp(qt_ref, c), comp(qtn_ref, c), max_ref.at[c * ring:(c + 1) * ring], k_blk, vt_blk,
                            bufs[c], acc) for c, acc in enumerate((acc1_ref, acc2_ref))], i, tk)

    lam = (jnp.exp(jnp.sum(lq1_ref[...] * lk1_ref[...], axis=-1, keepdims=True))
           - jnp.exp(jnp.sum(lq2_ref[...] * lk2_ref[...], axis=-1, keepdims=True)) + lambda_init)
    d = _normalised(acc1_ref, DIFF_V_DIM) - lam * _normalised(acc2_ref, DIFF_V_DIM)
    dn = d * lax.rsqrt(jnp.mean(d * d, axis=0, keepdims=True) + NORM_EPS)
    o_ref[...] = (dn.T * gsub_ref[...] * (1.0 - lambda_init)).astype(o_ref.dtype)


def _diff_attn_call(lq1, lk1, lq2, lk2, gsub, qt, k, vt, *, lambda_init):
    B, S, _ = k.shape
    tq, tk = QUERY_TILE, DIFF_KEY_BLOCK
    tile = pl.BlockSpec((None, tq, LANES), lambda b, h, i: (b, i, h))
    return pl.pallas_call(
        functools.partial(_diff_attn_kernel, lambda_init=lambda_init),
        grid=(B, DIFF_HEADS, S // tq),
        in_specs=[_const_spec(a.shape) for a in (lq1, lk1, lq2, lk2, gsub)] + [
            pl.BlockSpec((None, LANES, tq), lambda b, h, i: (b, h, i)),
            pl.BlockSpec((None, LANES, tq), lambda b, h, i: (b, h, jnp.minimum(i + 1, S // tq - 1))),
            pl.BlockSpec((None, S, LANES), lambda b, h, i: (b, 0, h)),
            pl.BlockSpec((None, S // tk, LANES, tk), lambda b, h, i: (b, 0, h, 0))],
        out_specs=tile,
        out_shape=jax.ShapeDtypeStruct((B, S, DIFF_HEADS * DIFF_V_DIM), BF16),
        scratch_shapes=_attn_scratch(tq, tk, DIFF_V_DIM),
        compiler_params=_params(3),
        name="diff_attn",
    )(lq1, lk1, lq2, lk2, gsub, qt, qt, k, vt)


def _post_kernel(x_ref, om_ref, od_ref, gate_ref, wbm_ref, wbd_ref, wo_ref, gffn_ref, wg_ref, wu_ref, wd_ref,
                 gfin_ref, out_ref):
    d_model = x_ref.shape[1]
    gates = gate_ref[...].astype(F32)
    merged = (gates[:, :d_model] * _dot(om_ref[...], wbm_ref[...])
              + gates[:, d_model:] * _dot(od_ref[...], wbd_ref[...]))
    x = x_ref[...] + _dot(merged.astype(BF16), wo_ref[...])
    hn = _rms(x, gffn_ref[...]).astype(BF16)
    a = _dot(hn, wg_ref[...])
    hid = (a * (1.0 / (1.0 + jnp.exp(-a)))) * _dot(hn, wu_ref[...])
    x = x + _dot(hid.astype(BF16), wd_ref[...])
    out_ref[...] = _rms(x, gfin_ref[...])


def _post_call(x2, om, od, gates, wbm, wbd, wo, gffn, wg, wu, wd, gfin):
    T, D = x2.shape
    tm = POST_TILE
    row = lambda a: pl.BlockSpec((tm, a.shape[1]), lambda i: (i, 0))
    consts = (wbm, wbd, wo, gffn, wg, wu, wd, gfin)
    return pl.pallas_call(
        _post_kernel,
        grid=(T // tm,),
        in_specs=[row(a) for a in (x2, om, od, gates)] + [_const_spec(a.shape) for a in consts],
        out_specs=pl.BlockSpec((tm, D), lambda i: (i, 0)),
        out_shape=jax.ShapeDtypeStruct((T, D), F32),
        compiler_params=_params(1, POST_VMEM_LIMIT_BYTES),
        name="post",
    )(x2, om, od, gates, *consts)


def _inv_freq_column(rot_dim):
    half = rot_dim // 2
    return jnp.exp(-math.log(ROPE_THETA) * jnp.arange(half, dtype=F32) * (2.0 / rot_dim)).reshape(half, 1)


def _pad_last(a, width):
    return jnp.pad(a, [(0, 0)] * (a.ndim - 1) + [(0, width - a.shape[-1])])


def kernel(x, positions, norm_mix_g, w_in, b_gate, mla_q_norm_g, mla_w_uq, mla_kv_norm_g, mla_w_ukv, diff_lambda_q1, diff_lambda_k1, diff_lambda_q2, diff_lambda_k2, diff_subln_g, w_branch_mla, w_branch_diff, w_out, norm_ffn_g, w_ffn_gate, w_ffn_up, w_ffn_down, norm_final_g):
    B, S, D = x.shape
    T = B * S
    assert w_in.shape[0] == 1, "single-layer block"
    assert S % QUERY_TILE == 0 and S % PROJ_TILE == 0
    layer = 0
    lambda_init = 0.8 - 0.6 * math.exp(-0.3 * layer)
    row = lambda a: a.reshape(1, -1)

    w = w_in[layer]
    o_kv = MLA_Q_RANK + MLA_KV_RANK
    o_kr = o_kv + MLA_ROPE_DIM
    o_dv = o_kr + 2 * DIFF_HEADS * LANES
    o_gate = o_dv + DIFF_HEADS * DIFF_V_DIM
    o_dk = o_kr + DIFF_HEADS * LANES
    w1 = jnp.concatenate([w[:, :o_kv], w[:, o_gate:]], axis=1).astype(BF16)
    wkpet = jnp.pad(w[:, o_kv:o_kr].T, ((MLA_NOPE_DIM, LANES - MLA_QK_DIM), (0, 0))).astype(BF16)
    wdqt, wdkt, wdvt = (w[:, a:b].T.astype(BF16) for a, b in ((o_kr, o_dk), (o_dk, o_dv), (o_dv, o_gate)))
    wuqt = _pad_last(mla_w_uq[layer].reshape(MLA_Q_RANK, MLA_HEADS, MLA_QK_DIM), LANES)
    wuqt = wuqt.reshape(MLA_Q_RANK, MLA_HEADS * LANES).T.astype(BF16)
    wukv = mla_w_ukv[layer].reshape(MLA_KV_RANK, MLA_HEADS, MLA_NOPE_DIM + MLA_V_DIM)
    wuk = _pad_last(wukv[..., :MLA_NOPE_DIM], LANES).reshape(MLA_KV_RANK, MLA_HEADS * LANES).astype(BF16)
    wuvt = wukv[..., MLA_NOPE_DIM:].reshape(MLA_KV_RANK, MLA_HEADS * MLA_V_DIM).T.astype(BF16)

    qt, k, vt, dqt, dk, dvt, gates = _proj_call(
        x, positions.astype(F32).reshape(B, 1, S),
        (_inv_freq_column(MLA_ROPE_DIM), _inv_freq_column(DIFF_ROT_DIM), row(norm_mix_g[layer]), w1,
         row(mla_q_norm_g[layer]), wuqt, row(mla_kv_norm_g[layer]), wuk, wuvt, wkpet, wdqt, wdkt, wdvt,
         row(b_gate[layer])))

    o_mla = _mla_attn_call(qt, k, vt)
    o_diff = _diff_attn_call(row(diff_lambda_q1[layer]), row(diff_lambda_k1[layer]), row(diff_lambda_q2[layer]),
                             row(diff_lambda_k2[layer]), row(diff_subln_g[layer]), dqt, dk, dvt,
                             lambda_init=lambda_init)

    out = _post_call(x.reshape(T, D), o_mla.reshape(T, -1), o_diff.reshape(T, -1), gates.reshape(T, -1),
                     w_branch_mla[layer].astype(BF16), w_branch_diff[layer].astype(BF16), w_out[layer].astype(BF16),
                     row(norm_ffn_g[layer]), w_ffn_gate[layer].astype(BF16), w_ffn_up[layer].astype(BF16),
                     w_ffn_down[layer].astype(BF16), row(norm_final_g))
    return out.reshape(B, S, D)
```

```python
import functools
import math
from typing import Callable, NamedTuple

import jax
import jax.numpy as jnp
from jax import lax
from jax.experimental import pallas as pl
from jax.experimental.pallas import tpu as pltpu

F32 = jnp.float32
BF16 = jnp.bfloat16

ROPE_THETA = 500000.0
NORM_EPS = 1e-6
LANES = 128
LOG2E = math.log2(math.e)

MLA_HEADS = 8
MLA_NOPE_DIM = 64
MLA_ROPE_DIM = 32
MLA_QK_DIM = MLA_NOPE_DIM + MLA_ROPE_DIM
MLA_V_DIM = 64
MLA_Q_RANK = 384
MLA_KV_RANK = 256
DIFF_HEADS = 4
DIFF_HEAD_DIM = 64
DIFF_V_DIM = 2 * DIFF_HEAD_DIM
DIFF_ROT_DIM = DIFF_HEAD_DIM // 4

VMEM_LIMIT_BYTES = 48 * 1024 * 1024
POST_VMEM_LIMIT_BYTES = 56 * 1024 * 1024
POST_TILE = 512
PROJ_TILE = 512
QUERY_TILE = 1024
MLA_KEY_BLOCK = 256
DIFF_KEY_BLOCK = 512
SLAB_PAD_ROWS = 8
SUM_ROWS = 16

_C_QLAT = 0
_C_KVLAT = _C_QLAT + MLA_Q_RANK
_C_GATE = _C_KVLAT + MLA_KV_RANK


def _rms(x, g):
    return x * lax.rsqrt(jnp.mean(x * x, axis=-1, keepdims=True) + NORM_EPS) * g


def _dot(a, b):
    return jnp.dot(a, b, preferred_element_type=F32)


def _dot_nt(a, b):
    return lax.dot_general(a, b, (((1,), (1,)), ((), ())), preferred_element_type=F32)


def _rope_rows(xt, base, half, cos, sin):
    x1, x2 = xt[base:base + half], xt[base + half:base + 2 * half]
    return [x1 * cos - x2 * sin, x2 * cos + x1 * sin]


def _const_spec(shape):
    return pl.BlockSpec(shape, lambda *_: (0,) * len(shape), pipeline_mode=pl.Buffered(1))


def _params(n_grid_dims, vmem_limit_bytes=VMEM_LIMIT_BYTES):
    return pltpu.CompilerParams(dimension_semantics=("arbitrary",) * n_grid_dims, vmem_limit_bytes=vmem_limit_bytes)


def _proj_kernel(x_ref, pos_ref, fm_ref, fd_ref, gmix_ref, w1_ref, gq_ref, wuqt_ref, gkv_ref, wuk_ref, wuvt_ref,
                 wkpet_ref, wdqt_ref, wdkt_ref, wdvt_ref, bg_ref,
                 qt_out, k_out, vt_out, dqt_out, dk_out, dvt_out, gate_out, *, mla_scale, diff_scale):
    x = x_ref[...]
    xn = _rms(x, gmix_ref[...]).astype(BF16)
    pos = pos_ref[...]
    ang = fm_ref[...] * pos
    cos_m, sin_m = jnp.cos(ang), jnp.sin(ang)
    ang = fd_ref[...] * pos
    cos_d, sin_d = jnp.cos(ang), jnp.sin(ang)
    half_m, half_d = MLA_ROPE_DIM // 2, DIFF_ROT_DIM // 2

    def store_per_key_block(out_ref, xt):
        n_blocks, _, tk = out_ref.shape
        for kb in range(n_blocks):
            out_ref[kb] = xt[:, kb * tk:(kb + 1) * tk].astype(BF16)

    q_lat = _dot(xn, w1_ref[:, _C_QLAT:_C_QLAT + MLA_Q_RANK])
    qt = _dot_nt(wuqt_ref[...], _rms(q_lat, gq_ref[...]).astype(BF16))
    for h in range(MLA_HEADS):
        b0 = h * LANES
        slab = jnp.concatenate([qt[b0:b0 + MLA_NOPE_DIM]] + _rope_rows(qt, b0 + MLA_NOPE_DIM, half_m, cos_m, sin_m)
                               + [qt[b0 + MLA_QK_DIM:b0 + LANES]], axis=0)
        qt_out[b0:b0 + LANES, :] = (slab * mla_scale).astype(BF16)

    kv_lat = _dot(xn, w1_ref[:, _C_KVLAT:_C_KVLAT + MLA_KV_RANK])
    kvn = _rms(kv_lat, gkv_ref[...]).astype(BF16)
    k_nope = _dot(kvn, wuk_ref[...])
    kpt = _dot_nt(wkpet_ref[...], xn)
    k_pe = jnp.concatenate([kpt[:MLA_NOPE_DIM]] + _rope_rows(kpt, MLA_NOPE_DIM, half_m, cos_m, sin_m)
                           + [kpt[MLA_QK_DIM:]], axis=0).T
    for h in range(MLA_HEADS):
        sl = slice(h * LANES, (h + 1) * LANES)
        k_out[:, sl] = (k_nope[:, sl] + k_pe).astype(BF16)
    store_per_key_block(vt_out, _dot_nt(wuvt_ref[...], kvn))

    def diff_rope(xt, h):
        parts = []
        for c in range(2):
            base = h * LANES + c * DIFF_HEAD_DIM
            parts += _rope_rows(xt, base, half_d, cos_d, sin_d) + [xt[base + DIFF_ROT_DIM:base + DIFF_HEAD_DIM]]
        return jnp.concatenate(parts, axis=0)

    dqt = _dot_nt(wdqt_ref[...], xn)
    dkt = _dot_nt(wdkt_ref[...], xn)
    for h in range(DIFF_HEADS):
        sl = slice(h * LANES, (h + 1) * LANES)
        dqt_out[sl, :] = (diff_rope(dqt, h) * diff_scale).astype(BF16)
        dk_out[:, sl] = diff_rope(dkt, h).T.astype(BF16)
    store_per_key_block(dvt_out, _dot_nt(wdvt_ref[...], xn))

    gate_pre = _dot(xn, w1_ref[:, _C_GATE:]) + bg_ref[...]
    gate_out[...] = (1.0 / (1.0 + jnp.exp(-gate_pre))).astype(BF16)


def _proj_call(x, pos, consts):
    B, S, D = x.shape
    tm = PROJ_TILE
    n_gate = consts[-1].shape[1]
    row = lambda w: pl.BlockSpec((None, tm, w), lambda b, i: (b, i, 0))
    feat = lambda w: pl.BlockSpec((None, w, tm), lambda b, i: (b, 0, i))
    col = lambda w, tk: pl.BlockSpec((None, tm // tk, w, tk), lambda b, i: (b, i, 0, 0))
    row_w = lambda w: jax.ShapeDtypeStruct((B, S, w), BF16)
    feat_w = lambda w: jax.ShapeDtypeStruct((B, w, S), BF16)
    col_w = lambda w, tk: jax.ShapeDtypeStruct((B, S // tk, w, tk), BF16)
    wq, wv, wd = MLA_HEADS * LANES, MLA_HEADS * MLA_V_DIM, DIFF_HEADS * LANES
    return pl.pallas_call(
        functools.partial(_proj_kernel, mla_scale=MLA_QK_DIM ** -0.5 * LOG2E, diff_scale=DIFF_HEAD_DIM ** -0.5 * LOG2E),
        grid=(B, S // tm),
        in_specs=[row(D), feat(1)] + [_const_spec(a.shape) for a in consts],
        out_specs=[feat(wq), row(wq), col(wv, MLA_KEY_BLOCK), feat(wd), row(wd), col(wd, DIFF_KEY_BLOCK), row(n_gate)],
        out_shape=[feat_w(wq), row_w(wq), col_w(wv, MLA_KEY_BLOCK), feat_w(wd), row_w(wd), col_w(wd, DIFF_KEY_BLOCK),
                   row_w(n_gate)],
        compiler_params=_params(2),
        name="proj",
    )(x, pos, *consts)


class _Stream(NamedTuple):
    qt: jax.Array
    qt_next: jax.Array
    max_ref: object
    k_blk: Callable
    vt_blk: Callable
    s_refs: tuple
    acc_ref: object


def _flash_streams(streams, i, tk):
    tq = streams[0].qt.shape[1]
    ring = tq // tk
    n_slabs = tq // LANES

    def store_cols(ref, x, c0=0):
        for c in range(c0 // LANES, n_slabs):
            ref[c, :tk, :] = x[:, c * LANES - c0:(c + 1) * LANES - c0]

    def load_cols(ref, c0=0):
        return jnp.concatenate([ref[c, :tk, :] for c in range(c0 // LANES, n_slabs)], axis=1)

    def qk(st, j, r, qt=None, c0=0):
        s = _dot(st.k_blk(j), (st.qt if qt is None else qt)[:, c0:])
        store_cols(st.s_refs[r], s, c0)
        return jnp.max(s, axis=0, keepdims=True)

    def softmax(st, r, col_max, m):
        m_new = jnp.maximum(m, col_max)
        p = jnp.exp2(load_cols(st.s_refs[r]) - m_new).astype(BF16)
        return m_new, jnp.exp2(m - m_new), p

    ones = jnp.ones((SUM_ROWS, tk), BF16)

    def pv(st, j, p):
        return _dot(jnp.concatenate([st.vt_blk(j), ones], axis=0), p)

    def start(st):
        st.acc_ref[...] = jnp.zeros(st.acc_ref.shape, F32)
        return jnp.full((1, tq), -jnp.inf, F32), tuple(st.max_ref[r:r + 1, :] for r in range(ring))

    def trip(st, t, carry):
        m, maxes = carry
        j0 = ring * t
        new_maxes = []
        for r in range(ring):
            m, alpha, p = softmax(st, r, maxes[r], m)
            new_maxes.append(qk(st, j0 + ring + r, r))
            st.acc_ref[...] = alpha * st.acc_ref[...] + pv(st, j0 + r, p)
        return m, tuple(new_maxes)

    def last_trip(st, carry):
        m, maxes = carry
        j0 = ring * (i - 1)
        for r in range(ring):
            m, alpha, p = softmax(st, r, maxes[r], m)
            qk(st, j0 + ring + r, r, c0=r * tk)
            st.acc_ref[...] = alpha * st.acc_ref[...] + pv(st, j0 + r, p)
        return m

    def finish(st, m):
        j0 = ring * i
        for d in range(ring):
            c0 = d * tk
            w = tq - c0
            mask = (lax.broadcasted_iota(jnp.int32, (tk, w), 0) <= lax.broadcasted_iota(jnp.int32, (tk, w), 1))
            s = jnp.where(mask, load_cols(st.s_refs[d], c0), -jnp.inf)
            st.max_ref[d:d + 1, :] = qk(st, d, d, st.qt_next)
            m_old = m[:, c0:]
            m_new = jnp.maximum(m_old, jnp.max(s, axis=0, keepdims=True))
            p = jnp.exp2(s - m_new).astype(BF16)
            alpha = jnp.exp2(m_old - m_new)
            m = m_new if d == 0 else jnp.concatenate([m[:, :c0], m_new], axis=1)
            st.acc_ref[:, c0:] = alpha * st.acc_ref[:, c0:] + pv(st, j0 + d, p)

    @pl.when(i == 0)
    def _():
        for st in streams:
            for r in range(ring):
                st.max_ref[r:r + 1, :] = qk(st, r, r)

    carries = lax.fori_loop(0, jnp.maximum(i - 1, 0), lambda t, cs: tuple(trip(st, t, c) for st, c in zip(streams, cs)),
                            tuple(start(st) for st in streams))
    ms = lax.cond(i > 0, lambda cs: tuple(last_trip(st, c) for st, c in zip(streams, cs)),
                  lambda cs: tuple(c[0] for c in cs), carries)
    for st, m in zip(streams, ms):
        finish(st, m)


def _normalised(acc_ref, dv):
    return acc_ref[:dv, :] / acc_ref[dv:dv + 1, :]


def _key_rows(j, tk):
    return pl.ds(pl.multiple_of(j * tk, tk), tk)


def _split_scratch(scratch, n_streams, ring):
    n = n_streams * ring
    return [scratch[k * ring:(k + 1) * ring] for k in range(n_streams)], scratch[n:]


def _attn_scratch(tq, tk, dv):
    ring = tq // tk
    return ([pltpu.VMEM((tq // LANES, tk + SLAB_PAD_ROWS, LANES), F32)] * (2 * ring)
            + [pltpu.VMEM((dv + SUM_ROWS, tq), F32)] * 2 + [pltpu.VMEM((2 * ring, tq), F32)])


def _mla_attn_kernel(qt_ref, qtn_ref, k0_ref, k1_ref, vt_ref, o_ref, *scratch):
    i = pl.program_id(2)
    tk = MLA_KEY_BLOCK
    ring = QUERY_TILE // tk
    bufs, (*accs, max_ref) = _split_scratch(scratch, 2, ring)
    streams = []
    for hh, (k_ref, s_refs, acc_ref) in enumerate(zip((k0_ref, k1_ref), bufs, accs)):
        rows = slice(hh * MLA_V_DIM, (hh + 1) * MLA_V_DIM)
        sl = slice(hh * LANES, (hh + 1) * LANES)
        streams.append(_Stream(qt_ref[sl, :], qtn_ref[sl, :], max_ref.at[hh * ring:(hh + 1) * ring],
                               lambda j, k_ref=k_ref: k_ref[_key_rows(j, tk), :],
                               lambda j, rows=rows: vt_ref[j, rows, :], s_refs, acc_ref))
    _flash_streams(streams, i, tk)
    outs = [_normalised(acc_ref, MLA_V_DIM) for acc_ref in accs]
    o_ref[...] = jnp.concatenate(outs, axis=0).T.astype(o_ref.dtype)


def _mla_attn_call(qt, k, vt):
    B, S, _ = k.shape
    tq, tk = QUERY_TILE, MLA_KEY_BLOCK
    head_keys = lambda hh: pl.BlockSpec((None, S, LANES), lambda b, h, i: (b, 0, 2 * h + hh))
    return pl.pallas_call(
        _mla_attn_kernel,
        grid=(B, MLA_HEADS // 2, S // tq),
        in_specs=[pl.BlockSpec((None, 2 * LANES, tq), lambda b, h, i: (b, h, i)),
                  pl.BlockSpec((None, 2 * LANES, tq), lambda b, h, i: (b, h, jnp.minimum(i + 1, S // tq - 1))),
                  head_keys(0), head_keys(1),
                  pl.BlockSpec((None, S // tk, LANES, tk), lambda b, h, i: (b, 0, h, 0))],
        out_specs=pl.BlockSpec((None, tq, LANES), lambda b, h, i: (b, i, h)),
        out_shape=jax.ShapeDtypeStruct((B, S, MLA_HEADS * MLA_V_DIM), BF16),
        scratch_shapes=_attn_scratch(tq, tk, MLA_V_DIM),
        compiler_params=_params(3),
        name="mla_attn",
    )(qt, qt, k, k, vt)


def _diff_attn_kernel(lq1_ref, lk1_ref, lq2_ref, lk2_ref, gsub_ref, qt_ref, qtn_ref, k_ref, vt_ref, o_ref, *scratch,
                      lambda_init):
    i = pl.program_id(2)
    tk = DIFF_KEY_BLOCK
    ring = QUERY_TILE // tk
    bufs, (acc1_ref, acc2_ref, max_ref) = _split_scratch(scratch, 2, ring)
    feat = lax.broadcasted_iota(jnp.int32, qt_ref.shape, 0)
    k_blk = lambda j: k_ref[_key_rows(j, tk), :]
    vt_blk = lambda j: vt_ref[j]

    def comp(ref, c):
        qt = ref[...]
        return jnp.where((feat >= DIFF_HEAD_DIM) == bool(c), qt, jnp.zeros_like(qt))

    _flash_streams([_Stream(comp(qt_ref, c), comp(qtn_ref, c), max_ref.at[c * ring:(c + 1) * ring], k_blk, vt_blk,
                            bufs[c], acc) for c, acc in enumerate((acc1_ref, acc2_ref))], i, tk)

    lam = (jnp.exp(jnp.sum(lq1_ref[...] * lk1_ref[...], axis=-1, keepdims=True))
           - jnp.exp(jnp.sum(lq2_ref[...] * lk2_ref[...], axis=-1, keepdims=True)) + lambda_init)
    d = _normalised(acc1_ref, DIFF_V_DIM) - lam * _normalised(acc2_ref, DIFF_V_DIM)
    dn = d * lax.rsqrt(jnp.mean(d * d, axis=0, keepdims=True) + NORM_EPS)
    o_ref[...] = (dn.T * gsub_ref[...] * (1.0 - lambda_init)).astype(o_ref.dtype)


def _diff_attn_call(lq1, lk1, lq2, lk2, gsub, qt, k, vt, *, lambda_init):
    B, S, _ = k.shape
    tq, tk = QUERY_TILE, DIFF_KEY_BLOCK
    tile = pl.BlockSpec((None, tq, LANES), lambda b, h, i: (b, i, h))
    return pl.pallas_call(
        functools.partial(_diff_attn_kernel, lambda_init=lambda_init),
        grid=(B, DIFF_HEADS, S // tq),
        in_specs=[_const_spec(a.shape) for a in (lq1, lk1, lq2, lk2, gsub)] + [
            pl.BlockSpec((None, LANES, tq), lambda b, h, i: (b, h, i)),
            pl.BlockSpec((None, LANES, tq), lambda b, h, i: (b, h, jnp.minimum(i + 1, S // tq - 1))),
            pl.BlockSpec((None, S, LANES), lambda b, h, i: (b, 0, h)),
            pl.BlockSpec((None, S // tk, LANES, tk), lambda b, h, i: (b, 0, h, 0))],
        out_specs=tile,
        out_shape=jax.ShapeDtypeStruct((B, S, DIFF_HEADS * DIFF_V_DIM), BF16),
        scratch_shapes=_attn_scratch(tq, tk, DIFF_V_DIM),
        compiler_params=_params(3),
        name="diff_attn",
    )(lq1, lk1, lq2, lk2, gsub, qt, qt, k, vt)


def _post_kernel(x_ref, om_ref, od_ref, gate_ref, wbm_ref, wbd_ref, wo_ref, gffn_ref, wg_ref, wu_ref, wd_ref,
                 gfin_ref, out_ref):
    d_model = x_ref.shape[1]
    gates = gate_ref[...].astype(F32)
    merged = (gates[:, :d_model] * _dot(om_ref[...], wbm_ref[...])
              + gates[:, d_model:] * _dot(od_ref[...], wbd_ref[...]))
    x = x_ref[...] + _dot(merged.astype(BF16), wo_ref[...])
    hn = _rms(x, gffn_ref[...]).astype(BF16)
    a = _dot(hn, wg_ref[...])
    hid = (a * (1.0 / (1.0 + jnp.exp(-a)))) * _dot(hn, wu_ref[...])
    x = x + _dot(hid.astype(BF16), wd_ref[...])
    out_ref[...] = _rms(x, gfin_ref[...])


def _post_call(x2, om, od, gates, wbm, wbd, wo, gffn, wg, wu, wd, gfin):
    T, D = x2.shape
    tm = POST_TILE
    row = lambda a: pl.BlockSpec((tm, a.shape[1]), lambda i: (i, 0))
    consts = (wbm, wbd, wo, gffn, wg, wu, wd, gfin)
    return pl.pallas_call(
        _post_kernel,
        grid=(T // tm,),
        in_specs=[row(a) for a in (x2, om, od, gates)] + [_const_spec(a.shape) for a in consts],
        out_specs=pl.BlockSpec((tm, D), lambda i: (i, 0)),
        out_shape=jax.ShapeDtypeStruct((T, D), F32),
        compiler_params=_params(1, POST_VMEM_LIMIT_BYTES),
        name="post",
    )(x2, om, od, gates, *consts)


def _inv_freq_column(rot_dim):
    half = rot_dim // 2
    return jnp.exp(-math.log(ROPE_THETA) * jnp.arange(half, dtype=F32) * (2.0 / rot_dim)).reshape(half, 1)


def _pad_last(a, width):
    return jnp.pad(a, [(0, 0)] * (a.ndim - 1) + [(0, width - a.shape[-1])])


def kernel(x, positions, norm_mix_g, w_in, b_gate, mla_q_norm_g, mla_w_uq, mla_kv_norm_g, mla_w_ukv, diff_lambda_q1, diff_lambda_k1, diff_lambda_q2, diff_lambda_k2, diff_subln_g, w_branch_mla, w_branch_diff, w_out, norm_ffn_g, w_ffn_gate, w_ffn_up, w_ffn_down, norm_final_g):
    B, S, D = x.shape
    T = B * S
    assert w_in.shape[0] == 1, "single-layer block"
    assert S % QUERY_TILE == 0 and S % PROJ_TILE == 0
    layer = 0
    lambda_init = 0.8 - 0.6 * math.exp(-0.3 * layer)
    row = lambda a: a.reshape(1, -1)

    w = w_in[layer]
    o_kv = MLA_Q_RANK + MLA_KV_RANK
    o_kr = o_kv + MLA_ROPE_DIM
    o_dv = o_kr + 2 * DIFF_HEADS * LANES
    o_gate = o_dv + DIFF_HEADS * DIFF_V_DIM
    o_dk = o_kr + DIFF_HEADS * LANES
    w1 = jnp.concatenate([w[:, :o_kv], w[:, o_gate:]], axis=1).astype(BF16)
    wkpet = jnp.pad(w[:, o_kv:o_kr].T, ((MLA_NOPE_DIM, LANES - MLA_QK_DIM), (0, 0))).astype(BF16)
    wdqt, wdkt, wdvt = (w[:, a:b].T.astype(BF16) for a, b in ((o_kr, o_dk), (o_dk, o_dv), (o_dv, o_gate)))
    wuqt = _pad_last(mla_w_uq[layer].reshape(MLA_Q_RANK, MLA_HEADS, MLA_QK_DIM), LANES)
    wuqt = wuqt.reshape(MLA_Q_RANK, MLA_HEADS * LANES).T.astype(BF16)
    wukv = mla_w_ukv[layer].reshape(MLA_KV_RANK, MLA_HEADS, MLA_NOPE_DIM + MLA_V_DIM)
    wuk = _pad_last(wukv[..., :MLA_NOPE_DIM], LANES).reshape(MLA_KV_RANK, MLA_HEADS * LANES).astype(BF16)
    wuvt = wukv[..., MLA_NOPE_DIM:].reshape(MLA_KV_RANK, MLA_HEADS * MLA_V_DIM).T.astype(BF16)

    qt, k, vt, dqt, dk, dvt, gates = _proj_call(
        x, positions.astype(F32).reshape(B, 1, S),
        (_inv_freq_column(MLA_ROPE_DIM), _inv_freq_column(DIFF_ROT_DIM), row(norm_mix_g[layer]), w1,
         row(mla_q_norm_g[layer]), wuqt, row(mla_kv_norm_g[layer]), wuk, wuvt, wkpet, wdqt, wdkt, wdvt,
         row(b_gate[layer])))

    o_mla = _mla_attn_call(qt, k, vt)
    o_diff = _diff_attn_call(row(diff_lambda_q1[layer]), row(diff_lambda_k1[layer]), row(diff_lambda_q2[layer]),
                             row(diff_lambda_k2[layer]), row(diff_subln_g[layer]), dqt, dk, dvt,
                             lambda_init=lambda_init)

    out = _post_call(x.reshape(T, D), o_mla.reshape(T, -1), o_diff.reshape(T, -1), gates.reshape(T, -1),
                     w_branch_mla[layer].astype(BF16), w_branch_diff[layer].astype(BF16), w_out[layer].astype(BF16),
                     row(norm_ffn_g[layer]), w_ffn_gate[layer].astype(BF16), w_ffn_up[layer].astype(BF16),
                     w_ffn_down[layer].astype(BF16), row(norm_final_g))
    return out.reshape(B, S, D)
```

```python
import functools
import math
from typing import Callable, NamedTuple

import jax
import jax.numpy as jnp
from jax import lax
from jax.experimental import pallas as pl
from jax.experimental.pallas import tpu as pltpu

F32 = jnp.float32
BF16 = jnp.bfloat16

ROPE_THETA = 500000.0
NORM_EPS = 1e-6
LANES = 128
LOG2E = math.log2(math.e)

MLA_HEADS = 8
MLA_NOPE_DIM = 64
MLA_ROPE_DIM = 32
MLA_QK_DIM = MLA_NOPE_DIM + MLA_ROPE_DIM
MLA_V_DIM = 64
MLA_Q_RANK = 384
MLA_KV_RANK = 256
DIFF_HEADS = 4
DIFF_HEAD_DIM = 64
DIFF_V_DIM = 2 * DIFF_HEAD_DIM
DIFF_ROT_DIM = DIFF_HEAD_DIM // 4

VMEM_LIMIT_BYTES = 48 * 1024 * 1024
POST_VMEM_LIMIT_BYTES = 56 * 1024 * 1024
POST_TILE = 512
PROJ_TILE = 512
QUERY_TILE = 1024
MLA_KEY_BLOCK = 256
DIFF_KEY_BLOCK = 512
SLAB_PAD_ROWS = 8
SUM_ROWS = 16

_C_QLAT = 0
_C_KVLAT = _C_QLAT + MLA_Q_RANK
_C_GATE = _C_KVLAT + MLA_KV_RANK


def _rms(x, g):
    return x * lax.rsqrt(jnp.mean(x * x, axis=-1, keepdims=True) + NORM_EPS) * g


def _dot(a, b):
    return jnp.dot(a, b, preferred_element_type=F32)


def _dot_nt(a, b):
    return lax.dot_general(a, b, (((1,), (1,)), ((), ())), preferred_element_type=F32)


def _rope_rows(xt, base, half, cos, sin):
    x1, x2 = xt[base:base + half], xt[base + half:base + 2 * half]
    return [x1 * cos - x2 * sin, x2 * cos + x1 * sin]


def _const_spec(shape):
    return pl.BlockSpec(shape, lambda *_: (0,) * len(shape), pipeline_mode=pl.Buffered(1))


def _params(n_grid_dims, vmem_limit_bytes=VMEM_LIMIT_BYTES):
    return pltpu.CompilerParams(dimension_semantics=("arbitrary",) * n_grid_dims, vmem_limit_bytes=vmem_limit_bytes)


def _proj_kernel(x_ref, pos_ref, fm_ref, fd_ref, gmix_ref, w1_ref, gq_ref, wuqt_ref, gkv_ref, wuk_ref, wuvt_ref,
                 wkpet_ref, wdqt_ref, wdkt_ref, wdvt_ref, bg_ref,
                 qt_out, k_out, vt_out, dqt_out, dk_out, dvt_out, gate_out, *, mla_scale, diff_scale):
    x = x_ref[...]
    xn = _rms(x, gmix_ref[...]).astype(BF16)
    pos = pos_ref[...]
    ang = fm_ref[...] * pos
    cos_m, sin_m = jnp.cos(ang), jnp.sin(ang)
    ang = fd_ref[...] * pos
    cos_d, sin_d = jnp.cos(ang), jnp.sin(ang)
    half_m, half_d = MLA_ROPE_DIM // 2, DIFF_ROT_DIM // 2

    def store_per_key_block(out_ref, xt):
        n_blocks, _, tk = out_ref.shape
        for kb in range(n_blocks):
            out_ref[kb] = xt[:, kb * tk:(kb + 1) * tk].astype(BF16)

    q_lat = _dot(xn, w1_ref[:, _C_QLAT:_C_QLAT + MLA_Q_RANK])
    qt = _dot_nt(wuqt_ref[...], _rms(q_lat, gq_ref[...]).astype(BF16))
    for h in range(MLA_HEADS):
        b0 = h * LANES
        slab = jnp.concatenate([qt[b0:b0 + MLA_NOPE_DIM]] + _rope_rows(qt, b0 + MLA_NOPE_DIM, half_m, cos_m, sin_m)
                               + [qt[b0 + MLA_QK_DIM:b0 + LANES]], axis=0)
        qt_out[b0:b0 + LANES, :] = (slab * mla_scale).astype(BF16)

    kv_lat = _dot(xn, w1_ref[:, _C_KVLAT:_C_KVLAT + MLA_KV_RANK])
    kvn = _rms(kv_lat, gkv_ref[...]).astype(BF16)
    k_nope = _dot(kvn, wuk_ref[...])
    kpt = _dot_nt(wkpet_ref[...], xn)
    k_pe = jnp.concatenate([kpt[:MLA_NOPE_DIM]] + _rope_rows(kpt, MLA_NOPE_DIM, half_m, cos_m, sin_m)
                           + [kpt[MLA_QK_DIM:]], axis=0).T
    for h in range(MLA_HEADS):
        sl = slice(h * LANES, (h + 1) * LANES)
        k_out[:, sl] = (k_nope[:, sl] + k_pe).astype(BF16)
    store_per_key_block(vt_out, _dot_nt(wuvt_ref[...], kvn))

    def diff_rope(xt, h):
        parts = []
        for c in range(2):
            base = h * LANES + c * DIFF_HEAD_DIM
            parts += _rope_rows(xt, base, half_d, cos_d, sin_d) + [xt[base + DIFF_ROT_DIM:base + DIFF_HEAD_DIM]]
        return jnp.concatenate(parts, axis=0)

    dqt = _dot_nt(wdqt_ref[...], xn)
    dkt = _dot_nt(wdkt_ref[...], xn)
    for h in range(DIFF_HEADS):
        sl = slice(h * LANES, (h + 1) * LANES)
        dqt_out[sl, :] = (diff_rope(dqt, h) * diff_scale).astype(BF16)
        dk_out[:, sl] = diff_rope(dkt, h).T.astype(BF16)
    store_per_key_block(dvt_out, _dot_nt(wdvt_ref[...], xn))

    gate_pre = _dot(xn, w1_ref[:, _C_GATE:]) + bg_ref[...]
    gate_out[...] = (1.0 / (1.0 + jnp.exp(-gate_pre))).astype(BF16)


def _proj_call(x, pos, consts):
    B, S, D = x.shape
    tm = PROJ_TILE
    n_gate = consts[-1].shape[1]
    row = lambda w: pl.BlockSpec((None, tm, w), lambda b, i: (b, i, 0))
    feat = lambda w: pl.BlockSpec((None, w, tm), lambda b, i: (b, 0, i))
    col = lambda w, tk: pl.BlockSpec((None, tm // tk, w, tk), lambda b, i: (b, i, 0, 0))
    row_w = lambda w: jax.ShapeDtypeStruct((B, S, w), BF16)
    feat_w = lambda w: jax.ShapeDtypeStruct((B, w, S), BF16)
    col_w = lambda w, tk: jax.ShapeDtypeStruct((B, S // tk, w, tk), BF16)
    wq, wv, wd = MLA_HEADS * LANES, MLA_HEADS * MLA_V_DIM, DIFF_HEADS * LANES
    return pl.pallas_call(
        functools.partial(_proj_kernel, mla_scale=MLA_QK_DIM ** -0.5 * LOG2E, diff_scale=DIFF_HEAD_DIM ** -0.5 * LOG2E),
        grid=(B, S // tm),
        in_specs=[row(D), feat(1)] + [_const_spec(a.shape) for a in consts],
        out_specs=[feat(wq), row(wq), col(wv, MLA_KEY_BLOCK), feat(wd), row(wd), col(wd, DIFF_KEY_BLOCK), row(n_gate)],
        out_shape=[feat_w(wq), row_w(wq), col_w(wv, MLA_KEY_BLOCK), feat_w(wd), row_w(wd), col_w(wd, DIFF_KEY_BLOCK),
                   row_w(n_gate)],
        compiler_params=_params(2),
        name="proj",
    )(x, pos, *consts)


class _Stream(NamedTuple):
    qt: jax.Array
    qt_next: jax.Array
    max_ref: object
    k_blk: Callable
    vt_blk: Callable
    s_refs: tuple
    acc_ref: object


def _flash_streams(streams, i, tk):
    tq = streams[0].qt.shape[1]
    ring = tq // tk
    n_slabs = tq // LANES

    def store_cols(ref, x, c0=0):
        for c in range(c0 // LANES, n_slabs):
            ref[c, :tk, :] = x[:, c * LANES - c0:(c + 1) * LANES - c0]

    def load_cols(ref, c0=0):
        return jnp.concatenate([ref[c, :tk, :] for c in range(c0 // LANES, n_slabs)], axis=1)

    def qk(st, j, r, qt=None, c0=0):
        s = _dot(st.k_blk(j), (st.qt if qt is None else qt)[:, c0:])
        store_cols(st.s_refs[r], s, c0)
        return jnp.max(s, axis=0, keepdims=True)

    def softmax(st, r, col_max, m):
        m_new = jnp.maximum(m, col_max)
        p = jnp.exp2(load_cols(st.s_refs[r]) - m_new).astype(BF16)
        return m_new, jnp.exp2(m - m_new), p

    ones = jnp.ones((SUM_ROWS, tk), BF16)

    def pv(st, j, p):
        return _dot(jnp.concatenate([st.vt_blk(j), ones], axis=0), p)

    def start(st):
        st.acc_ref[...] = jnp.zeros(st.acc_ref.shape, F32)
        return jnp.full((1, tq), -jnp.inf, F32), tuple(st.max_ref[r:r + 1, :] for r in range(ring))

    def trip(st, t, carry):
        m, maxes = carry
        j0 = ring * t
        new_maxes = []
        for r in range(ring):
            m, alpha, p = softmax(st, r, maxes[r], m)
            new_maxes.append(qk(st, j0 + ring + r, r))
            st.acc_ref[...] = alpha * st.acc_ref[...] + pv(st, j0 + r, p)
        return m, tuple(new_maxes)

    def last_trip(st, carry):
        m, maxes = carry
        j0 = ring * (i - 1)
        for r in range(ring):
            m, alpha, p = softmax(st, r, maxes[r], m)
            qk(st, j0 + ring + r, r, c0=r * tk)
            st.acc_ref[...] = alpha * st.acc_ref[...] + pv(st, j0 + r, p)
        return m

    def finish(st, m, prefetch):
        j0 = ring * i
        for d in range(ring):
            c0 = d * tk
            w = tq - c0
            mask = (lax.broadcasted_iota(jnp.int32, (tk, w), 0) <= lax.broadcasted_iota(jnp.int32, (tk, w), 1))
            s = jnp.where(mask, load_cols(st.s_refs[d], c0), -jnp.inf)
            if prefetch:
                st.max_ref[d:d + 1, :] = qk(st, d, d, st.qt_next)
            m_old = m[:, c0:]
            m_new = jnp.maximum(m_old, jnp.max(s, axis=0, keepdims=True))
            p = jnp.exp2(s - m_new).astype(BF16)
            alpha = jnp.exp2(m_old - m_new)
            m = m_new if d == 0 else jnp.concatenate([m[:, :c0], m_new], axis=1)
            st.acc_ref[:, c0:] = alpha * st.acc_ref[:, c0:] + pv(st, j0 + d, p)

    @pl.when(i == 0)
    def _():
        for st in streams:
            for r in range(ring):
                st.max_ref[r:r + 1, :] = qk(st, r, r)

    carries = lax.fori_loop(0, jnp.maximum(i - 1, 0), lambda t, cs: tuple(trip(st, t, c) for st, c in zip(streams, cs)),
                            tuple(start(st) for st in streams))
    ms = lax.cond(i > 0, lambda cs: tuple(last_trip(st, c) for st, c in zip(streams, cs)),
                  lambda cs: tuple(c[0] for c in cs), carries)
    for prefetch in (True, False):
        @pl.when((i < pl.num_programs(2) - 1) == prefetch)
        def _(prefetch=prefetch):
            for st, m in zip(streams, ms):
                finish(st, m, prefetch)


def _normalised(acc_ref, dv):
    return acc_ref[:dv, :] / acc_ref[dv:dv + 1, :]


def _key_rows(j, tk):
    return pl.ds(pl.multiple_of(j * tk, tk), tk)


def _split_scratch(scratch, n_streams, ring):
    n = n_streams * ring
    return [scratch[k * ring:(k + 1) * ring] for k in range(n_streams)], scratch[n:]


def _attn_scratch(tq, tk, dv):
    ring = tq // tk
    return ([pltpu.VMEM((tq // LANES, tk + SLAB_PAD_ROWS, LANES), F32)] * (2 * ring)
            + [pltpu.VMEM((dv + SUM_ROWS, tq), F32)] * 2 + [pltpu.VMEM((2 * ring, tq), F32)])


def _mla_attn_kernel(qt_ref, qtn_ref, k0_ref, k1_ref, vt_ref, o_ref, *scratch):
    i = pl.program_id(2)
    tk = MLA_KEY_BLOCK
    ring = QUERY_TILE // tk
    bufs, (*accs, max_ref) = _split_scratch(scratch, 2, ring)
    streams = []
    for hh, (k_ref, s_refs, acc_ref) in enumerate(zip((k0_ref, k1_ref), bufs, accs)):
        rows = slice(hh * MLA_V_DIM, (hh + 1) * MLA_V_DIM)
        sl = slice(hh * LANES, (hh + 1) * LANES)
        streams.append(_Stream(qt_ref[sl, :], qtn_ref[sl, :], max_ref.at[hh * ring:(hh + 1) * ring],
                               lambda j, k_ref=k_ref: k_ref[_key_rows(j, tk), :],
                               lambda j, rows=rows: vt_ref[j, rows, :], s_refs, acc_ref))
    _flash_streams(streams, i, tk)
    outs = [_normalised(acc_ref, MLA_V_DIM) for acc_ref in accs]
    o_ref[...] = jnp.concatenate(outs, axis=0).T.astype(o_ref.dtype)


def _mla_attn_call(qt, k, vt):
    B, S, _ = k.shape
    tq, tk = QUERY_TILE, MLA_KEY_BLOCK
    head_keys = lambda hh: pl.BlockSpec((None, S, LANES), lambda b, h, i: (b, 0, 2 * h + hh))
    return pl.pallas_call(
        _mla_attn_kernel,
        grid=(B, MLA_HEADS // 2, S // tq),
        in_specs=[pl.BlockSpec((None, 2 * LANES, tq), lambda b, h, i: (b, h, i)),
                  pl.BlockSpec((None, 2 * LANES, tq), lambda b, h, i: (b, h, jnp.minimum(i + 1, S // tq - 1))),
                  head_keys(0), head_keys(1),
                  pl.BlockSpec((None, S // tk, LANES, tk), lambda b, h, i: (b, 0, h, 0))],
        out_specs=pl.BlockSpec((None, tq, LANES), lambda b, h, i: (b, i, h)),
        out_shape=jax.ShapeDtypeStruct((B, S, MLA_HEADS * MLA_V_DIM), BF16),
        scratch_shapes=_attn_scratch(tq, tk, MLA_V_DIM),
        compiler_params=_params(3),
        name="mla_attn",
    )(qt, qt, k, k, vt)


def _diff_attn_kernel(lq1_ref, lk1_ref, lq2_ref, lk2_ref, gsub_ref, qt_ref, qtn_ref, k_ref, vt_ref, o_ref, *scratch,
                      lambda_init):
    i = pl.program_id(2)
    tk = DIFF_KEY_BLOCK
    ring = QUERY_TILE // tk
    bufs, (acc1_ref, acc2_ref, max_ref) = _split_scratch(scratch, 2, ring)
    feat = lax.broadcasted_iota(jnp.int32, qt_ref.shape, 0)
    k_blk = lambda j: k_ref[_key_rows(j, tk), :]
    vt_blk = lambda j: vt_ref[j]

    def comp(ref, c):
        qt = ref[...]
        return jnp.where((feat >= DIFF_HEAD_DIM) == bool(c), qt, jnp.zeros_like(qt))

    _flash_streams([_Stream(comp(qt_ref, c), comp(qtn_ref, c), max_ref.at[c * ring:(c + 1) * ring], k_blk, vt_blk,
                            bufs[c], acc) for c, acc in enumerate((acc1_ref, acc2_ref))], i, tk)

    lam = (jnp.exp(jnp.sum(lq1_ref[...] * lk1_ref[...], axis=-1, keepdims=True))
           - jnp.exp(jnp.sum(lq2_ref[...] * lk2_ref[...], axis=-1, keepdims=True)) + lambda_init)
    d = _normalised(acc1_ref, DIFF_V_DIM) - lam * _normalised(acc2_ref, DIFF_V_DIM)
    dn = d * lax.rsqrt(jnp.mean(d * d, axis=0, keepdims=True) + NORM_EPS)
    o_ref[...] = (dn.T * gsub_ref[...] * (1.0 - lambda_init)).astype(o_ref.dtype)


def _diff_attn_call(lq1, lk1, lq2, lk2, gsub, qt, k, vt, *, lambda_init):
    B, S, _ = k.shape
    tq, tk = QUERY_TILE, DIFF_KEY_BLOCK
    tile = pl.BlockSpec((None, tq, LANES), lambda b, h, i: (b, i, h))
    return pl.pallas_call(
        functools.partial(_diff_attn_kernel, lambda_init=lambda_init),
        grid=(B, DIFF_HEADS, S // tq),
        in_specs=[_const_spec(a.shape) for a in (lq1, lk1, lq2, lk2, gsub)] + [
            pl.BlockSpec((None, LANES, tq), lambda b, h, i: (b, h, i)),
            pl.BlockSpec((None, LANES, tq), lambda b, h, i: (b, h, jnp.minimum(i + 1, S // tq - 1))),
            pl.BlockSpec((None, S, LANES), lambda b, h, i: (b, 0, h)),
            pl.BlockSpec((None, S // tk, LANES, tk), lambda b, h, i: (b, 0, h, 0))],
        out_specs=tile,
        out_shape=jax.ShapeDtypeStruct((B, S, DIFF_HEADS * DIFF_V_DIM), BF16),
        scratch_shapes=_attn_scratch(tq, tk, DIFF_V_DIM),
        compiler_params=_params(3),
        name="diff_attn",
    )(lq1, lk1, lq2, lk2, gsub, qt, qt, k, vt)


def _post_kernel(x_ref, om_ref, od_ref, gate_ref, wbm_ref, wbd_ref, wo_ref, gffn_ref, wg_ref, wu_ref, wd_ref,
                 gfin_ref, out_ref):
    d_model = x_ref.shape[1]
    gates = gate_ref[...].astype(F32)
    merged = (gates[:, :d_model] * _dot(om_ref[...], wbm_ref[...])
              + gates[:, d_model:] * _dot(od_ref[...], wbd_ref[...]))
    x = x_ref[...] + _dot(merged.astype(BF16), wo_ref[...])
    hn = _rms(x, gffn_ref[...]).astype(BF16)
    a = _dot(hn, wg_ref[...])
    hid = (a * (1.0 / (1.0 + jnp.exp(-a)))) * _dot(hn, wu_ref[...])
    x = x + _dot(hid.astype(BF16), wd_ref[...])
    out_ref[...] = _rms(x, gfin_ref[...])


def _post_call(x2, om, od, gates, wbm, wbd, wo, gffn, wg, wu, wd, gfin):
    T, D = x2.shape
    tm = POST_TILE
    row = lambda a: pl.BlockSpec((tm, a.shape[1]), lambda i: (i, 0))
    consts = (wbm, wbd, wo, gffn, wg, wu, wd, gfin)
    return pl.pallas_call(
        _post_kernel,
        grid=(T // tm,),
        in_specs=[row(a) for a in (x2, om, od, gates)] + [_const_spec(a.shape) for a in consts],
        out_specs=pl.BlockSpec((tm, D), lambda i: (i, 0)),
        out_shape=jax.ShapeDtypeStruct((T, D), F32),
        compiler_params=_params(1, POST_VMEM_LIMIT_BYTES),
        name="post",
    )(x2, om, od, gates, *consts)


def _inv_freq_column(rot_dim):
    half = rot_dim // 2
    return jnp.exp(-math.log(ROPE_THETA) * jnp.arange(half, dtype=F32) * (2.0 / rot_dim)).reshape(half, 1)


def _pad_last(a, width):
    return jnp.pad(a, [(0, 0)] * (a.ndim - 1) + [(0, width - a.shape[-1])])


def kernel(x, positions, norm_mix_g, w_in, b_gate, mla_q_norm_g, mla_w_uq, mla_kv_norm_g, mla_w_ukv, diff_lambda_q1, diff_lambda_k1, diff_lambda_q2, diff_lambda_k2, diff_subln_g, w_branch_mla, w_branch_diff, w_out, norm_ffn_g, w_ffn_gate, w_ffn_up, w_ffn_down, norm_final_g):
    B, S, D = x.shape
    T = B * S
    assert w_in.shape[0] == 1, "single-layer block"
    assert S % QUERY_TILE == 0 and S % PROJ_TILE == 0
    layer = 0
    lambda_init = 0.8 - 0.6 * math.exp(-0.3 * layer)
    row = lambda a: a.reshape(1, -1)

    w = w_in[layer]
    o_kv = MLA_Q_RANK + MLA_KV_RANK
    o_kr = o_kv + MLA_ROPE_DIM
    o_dv = o_kr + 2 * DIFF_HEADS * LANES
    o_gate = o_dv + DIFF_HEADS * DIFF_V_DIM
    o_dk = o_kr + DIFF_HEADS * LANES
    w1 = jnp.concatenate([w[:, :o_kv], w[:, o_gate:]], axis=1).astype(BF16)
    wkpet = jnp.pad(w[:, o_kv:o_kr].T, ((MLA_NOPE_DIM, LANES - MLA_QK_DIM), (0, 0))).astype(BF16)
    wdqt, wdkt, wdvt = (w[:, a:b].T.astype(BF16) for a, b in ((o_kr, o_dk), (o_dk, o_dv), (o_dv, o_gate)))
    wuqt = _pad_last(mla_w_uq[layer].reshape(MLA_Q_RANK, MLA_HEADS, MLA_QK_DIM), LANES)
    wuqt = wuqt.reshape(MLA_Q_RANK, MLA_HEADS * LANES).T.astype(BF16)
    wukv = mla_w_ukv[layer].reshape(MLA_KV_RANK, MLA_HEADS, MLA_NOPE_DIM + MLA_V_DIM)
    wuk = _pad_last(wukv[..., :MLA_NOPE_DIM], LANES).reshape(MLA_KV_RANK, MLA_HEADS * LANES).astype(BF16)
    wuvt = wukv[..., MLA_NOPE_DIM:].reshape(MLA_KV_RANK, MLA_HEADS * MLA_V_DIM).T.astype(BF16)

    qt, k, vt, dqt, dk, dvt, gates = _proj_call(
        x, positions.astype(F32).reshape(B, 1, S),
        (_inv_freq_column(MLA_ROPE_DIM), _inv_freq_column(DIFF_ROT_DIM), row(norm_mix_g[layer]), w1,
         row(mla_q_norm_g[layer]), wuqt, row(mla_kv_norm_g[layer]), wuk, wuvt, wkpet, wdqt, wdkt, wdvt,
         row(b_gate[layer])))

    o_mla = _mla_attn_call(qt, k, vt)
    o_diff = _diff_attn_call(row(diff_lambda_q1[layer]), row(diff_lambda_k1[layer]), row(diff_lambda_q2[layer]),
                             row(diff_lambda_k2[layer]), row(diff_subln_g[layer]), dqt, dk, dvt,
                             lambda_init=lambda_init)

    out = _post_call(x.reshape(T, D), o_mla.reshape(T, -1), o_diff.reshape(T, -1), gates.reshape(T, -1),
                     w_branch_mla[layer].astype(BF16), w_branch_diff[layer].astype(BF16), w_out[layer].astype(BF16),
                     row(norm_ffn_g[layer]), w_ffn_gate[layer].astype(BF16), w_ffn_up[layer].astype(BF16),
                     w_ffn_down[layer].astype(BF16), row(norm_final_g))
    return out.reshape(B, S, D)
```

```python
import functools
import math
from typing import Callable, NamedTuple

import jax
import jax.numpy as jnp
from jax import lax
from jax.experimental import pallas as pl
from jax.experimental.pallas import tpu as pltpu

F32 = jnp.float32
BF16 = jnp.bfloat16

ROPE_THETA = 500000.0
NORM_EPS = 1e-6
LANES = 128
LOG2E = math.log2(math.e)

MLA_HEADS = 8
MLA_NOPE_DIM = 64
MLA_ROPE_DIM = 32
MLA_QK_DIM = MLA_NOPE_DIM + MLA_ROPE_DIM
MLA_V_DIM = 64
MLA_Q_RANK = 384
MLA_KV_RANK = 256
DIFF_HEADS = 4
DIFF_HEAD_DIM = 64
DIFF_V_DIM = 2 * DIFF_HEAD_DIM
DIFF_ROT_DIM = DIFF_HEAD_DIM // 4

VMEM_LIMIT_BYTES = 48 * 1024 * 1024
POST_TILE = 512
FFN_CHUNK = 1024
PROJ_TILE = 512
QUERY_TILE = 1024
MLA_KEY_BLOCK = 256
DIFF_KEY_BLOCK = 512
SLAB_PAD_ROWS = 8
SUM_ROWS = 16

_C_QLAT = 0
_C_KVLAT = _C_QLAT + MLA_Q_RANK
_C_GATE = _C_KVLAT + MLA_KV_RANK


def _rms(x, g):
    return x * lax.rsqrt(jnp.mean(x * x, axis=-1, keepdims=True) + NORM_EPS) * g


def _dot(a, b):
    return jnp.dot(a, b, preferred_element_type=F32)


def _dot_nt(a, b):
    return lax.dot_general(a, b, (((1,), (1,)), ((), ())), preferred_element_type=F32)


def _rope_rows(xt, base, half, cos, sin):
    x1, x2 = xt[base:base + half], xt[base + half:base + 2 * half]
    return [x1 * cos - x2 * sin, x2 * cos + x1 * sin]


def _const_spec(shape):
    return pl.BlockSpec(shape, lambda *_: (0,) * len(shape), pipeline_mode=pl.Buffered(1))


def _params(n_grid_dims):
    return pltpu.CompilerParams(dimension_semantics=("arbitrary",) * n_grid_dims, vmem_limit_bytes=VMEM_LIMIT_BYTES)


def _proj_kernel(x_ref, pos_ref, fm_ref, fd_ref, gmix_ref, w1_ref, gq_ref, wuqt_ref, gkv_ref, wuk_ref, wuvt_ref,
                 wkpet_ref, wdqt_ref, wdkt_ref, wdvt_ref, bg_ref,
                 qt_out, k_out, vt_out, dqt_out, dk_out, dvt_out, gate_out, *, mla_scale, diff_scale):
    x = x_ref[...]
    xn = _rms(x, gmix_ref[...]).astype(BF16)
    pos = pos_ref[...]
    ang = fm_ref[...] * pos
    cos_m, sin_m = jnp.cos(ang), jnp.sin(ang)
    ang = fd_ref[...] * pos
    cos_d, sin_d = jnp.cos(ang), jnp.sin(ang)
    half_m, half_d = MLA_ROPE_DIM // 2, DIFF_ROT_DIM // 2

    def store_per_key_block(out_ref, xt):
        n_blocks, _, tk = out_ref.shape
        for kb in range(n_blocks):
            out_ref[kb] = xt[:, kb * tk:(kb + 1) * tk].astype(BF16)

    q_lat = _dot(xn, w1_ref[:, _C_QLAT:_C_QLAT + MLA_Q_RANK])
    qt = _dot_nt(wuqt_ref[...], _rms(q_lat, gq_ref[...]).astype(BF16))
    for h in range(MLA_HEADS):
        b0 = h * LANES
        slab = jnp.concatenate([qt[b0:b0 + MLA_NOPE_DIM]] + _rope_rows(qt, b0 + MLA_NOPE_DIM, half_m, cos_m, sin_m)
                               + [qt[b0 + MLA_QK_DIM:b0 + LANES]], axis=0)
        qt_out[b0:b0 + LANES, :] = (slab * mla_scale).astype(BF16)

    kv_lat = _dot(xn, w1_ref[:, _C_KVLAT:_C_KVLAT + MLA_KV_RANK])
    kvn = _rms(kv_lat, gkv_ref[...]).astype(BF16)
    k_nope = _dot(kvn, wuk_ref[...])
    kpt = _dot_nt(wkpet_ref[...], xn)
    k_pe = jnp.concatenate([kpt[:MLA_NOPE_DIM]] + _rope_rows(kpt, MLA_NOPE_DIM, half_m, cos_m, sin_m)
                           + [kpt[MLA_QK_DIM:]], axis=0).T
    for h in range(MLA_HEADS):
        sl = slice(h * LANES, (h + 1) * LANES)
        k_out[:, sl] = (k_nope[:, sl] + k_pe).astype(BF16)
    store_per_key_block(vt_out, _dot_nt(wuvt_ref[...], kvn))

    def diff_rope(xt, h):
        parts = []
        for c in range(2):
            base = h * LANES + c * DIFF_HEAD_DIM
            parts += _rope_rows(xt, base, half_d, cos_d, sin_d) + [xt[base + DIFF_ROT_DIM:base + DIFF_HEAD_DIM]]
        return jnp.concatenate(parts, axis=0)

    dqt = _dot_nt(wdqt_ref[...], xn)
    dkt = _dot_nt(wdkt_ref[...], xn)
    for h in range(DIFF_HEADS):
        sl = slice(h * LANES, (h + 1) * LANES)
        dqt_out[sl, :] = (diff_rope(dqt, h) * diff_scale).astype(BF16)
        dk_out[:, sl] = diff_rope(dkt, h).T.astype(BF16)
    store_per_key_block(dvt_out, _dot_nt(wdvt_ref[...], xn))

    gate_pre = _dot(xn, w1_ref[:, _C_GATE:]) + bg_ref[...]
    gate_out[...] = (1.0 / (1.0 + jnp.exp(-gate_pre))).astype(BF16)


def _proj_call(x, pos, consts):
    B, S, D = x.shape
    tm = PROJ_TILE
    n_gate = consts[-1].shape[1]
    row = lambda w: pl.BlockSpec((None, tm, w), lambda b, i: (b, i, 0))
    feat = lambda w: pl.BlockSpec((None, w, tm), lambda b, i: (b, 0, i))
    col = lambda w, tk: pl.BlockSpec((None, tm // tk, w, tk), lambda b, i: (b, i, 0, 0))
    row_w = lambda w: jax.ShapeDtypeStruct((B, S, w), BF16)
    feat_w = lambda w: jax.ShapeDtypeStruct((B, w, S), BF16)
    col_w = lambda w, tk: jax.ShapeDtypeStruct((B, S // tk, w, tk), BF16)
    wq, wv, wd = MLA_HEADS * LANES, MLA_HEADS * MLA_V_DIM, DIFF_HEADS * LANES
    return pl.pallas_call(
        functools.partial(_proj_kernel, mla_scale=MLA_QK_DIM ** -0.5 * LOG2E, diff_scale=DIFF_HEAD_DIM ** -0.5 * LOG2E),
        grid=(B, S // tm),
        in_specs=[row(D), feat(1)] + [_const_spec(a.shape) for a in consts],
        out_specs=[feat(wq), row(wq), col(wv, MLA_KEY_BLOCK), feat(wd), row(wd), col(wd, DIFF_KEY_BLOCK), row(n_gate)],
        out_shape=[feat_w(wq), row_w(wq), col_w(wv, MLA_KEY_BLOCK), feat_w(wd), row_w(wd), col_w(wd, DIFF_KEY_BLOCK),
                   row_w(n_gate)],
        compiler_params=_params(2),
        name="proj",
    )(x, pos, *consts)


class _Stream(NamedTuple):
    qt: jax.Array
    qt_next: jax.Array
    max_ref: object
    k_blk: Callable
    vt_blk: Callable
    s_refs: tuple
    acc_ref: object


def _flash_streams(streams, i, tk):
    tq = streams[0].qt.shape[1]
    ring = tq // tk
    n_slabs = tq // LANES

    def store_cols(ref, x, c0=0):
        for c in range(c0 // LANES, n_slabs):
            ref[c, :tk, :] = x[:, c * LANES - c0:(c + 1) * LANES - c0]

    def load_cols(ref, c0=0):
        return jnp.concatenate([ref[c, :tk, :] for c in range(c0 // LANES, n_slabs)], axis=1)

    def qk(st, j, r, qt=None, c0=0):
        s = _dot(st.k_blk(j), (st.qt if qt is None else qt)[:, c0:])
        store_cols(st.s_refs[r], s, c0)
        return jnp.max(s, axis=0, keepdims=True)

    def softmax(st, r, col_max, m):
        m_new = jnp.maximum(m, col_max)
        p = jnp.exp2(load_cols(st.s_refs[r]) - m_new).astype(BF16)
        return m_new, jnp.exp2(m - m_new), p

    ones = jnp.ones((SUM_ROWS, tk), BF16)

    def pv(st, j, p):
        return _dot(jnp.concatenate([st.vt_blk(j), ones], axis=0), p)

    def start(st):
        st.acc_ref[...] = jnp.zeros(st.acc_ref.shape, F32)
        return jnp.full((1, tq), -jnp.inf, F32), tuple(st.max_ref[r:r + 1, :] for r in range(ring))

    def trip(st, t, carry):
        m, maxes = carry
        j0 = ring * t
        new_maxes = []
        for r in range(ring):
            m, alpha, p = softmax(st, r, maxes[r], m)
            new_maxes.append(qk(st, j0 + ring + r, r))
            st.acc_ref[...] = alpha * st.acc_ref[...] + pv(st, j0 + r, p)
        return m, tuple(new_maxes)

    def last_trip(st, carry):
        m, maxes = carry
        j0 = ring * (i - 1)
        for r in range(ring):
            m, alpha, p = softmax(st, r, maxes[r], m)
            qk(st, j0 + ring + r, r, c0=r * tk)
            st.acc_ref[...] = alpha * st.acc_ref[...] + pv(st, j0 + r, p)
        return m

    def finish(st, m, prefetch):
        j0 = ring * i
        for d in range(ring):
            c0 = d * tk
            w = tq - c0
            mask = (lax.broadcasted_iota(jnp.int32, (tk, w), 0) <= lax.broadcasted_iota(jnp.int32, (tk, w), 1))
            s = jnp.where(mask, load_cols(st.s_refs[d], c0), -jnp.inf)
            if prefetch:
                st.max_ref[d:d + 1, :] = qk(st, d, d, st.qt_next)
            m_old = m[:, c0:]
            m_new = jnp.maximum(m_old, jnp.max(s, axis=0, keepdims=True))
            p = jnp.exp2(s - m_new).astype(BF16)
            alpha = jnp.exp2(m_old - m_new)
            m = m_new if d == 0 else jnp.concatenate([m[:, :c0], m_new], axis=1)
            st.acc_ref[:, c0:] = alpha * st.acc_ref[:, c0:] + pv(st, j0 + d, p)

    @pl.when(i == 0)
    def _():
        for st in streams:
            for r in range(ring):
                st.max_ref[r:r + 1, :] = qk(st, r, r)

    carries = lax.fori_loop(0, jnp.maximum(i - 1, 0), lambda t, cs: tuple(trip(st, t, c) for st, c in zip(streams, cs)),
                            tuple(start(st) for st in streams))
    ms = lax.cond(i > 0, lambda cs: tuple(last_trip(st, c) for st, c in zip(streams, cs)),
                  lambda cs: tuple(c[0] for c in cs), carries)
    for prefetch in (True, False):
        @pl.when((i < pl.num_programs(2) - 1) == prefetch)
        def _(prefetch=prefetch):
            for st, m in zip(streams, ms):
                finish(st, m, prefetch)


def _normalised(acc_ref, dv):
    return acc_ref[:dv, :] / acc_ref[dv:dv + 1, :]


def _key_rows(j, tk):
    return pl.ds(pl.multiple_of(j * tk, tk), tk)


def _split_scratch(scratch, n_streams, ring):
    n = n_streams * ring
    return [scratch[k * ring:(k + 1) * ring] for k in range(n_streams)], scratch[n:]


def _attn_scratch(tq, tk, dv):
    ring = tq // tk
    return ([pltpu.VMEM((tq // LANES, tk + SLAB_PAD_ROWS, LANES), F32)] * (2 * ring)
            + [pltpu.VMEM((dv + SUM_ROWS, tq), F32)] * 2 + [pltpu.VMEM((2 * ring, tq), F32)])


def _mla_attn_kernel(qt_ref, qtn_ref, k0_ref, k1_ref, vt_ref, o_ref, *scratch):
    i = pl.program_id(2)
    tk = MLA_KEY_BLOCK
    ring = QUERY_TILE // tk
    bufs, (*accs, max_ref) = _split_scratch(scratch, 2, ring)
    streams = []
    for hh, (k_ref, s_refs, acc_ref) in enumerate(zip((k0_ref, k1_ref), bufs, accs)):
        rows = slice(hh * MLA_V_DIM, (hh + 1) * MLA_V_DIM)
        sl = slice(hh * LANES, (hh + 1) * LANES)
        streams.append(_Stream(qt_ref[sl, :], qtn_ref[sl, :], max_ref.at[hh * ring:(hh + 1) * ring],
                               lambda j, k_ref=k_ref: k_ref[_key_rows(j, tk), :],
                               lambda j, rows=rows: vt_ref[j, rows, :], s_refs, acc_ref))
    _flash_streams(streams, i, tk)
    outs = [_normalised(acc_ref, MLA_V_DIM) for acc_ref in accs]
    o_ref[...] = jnp.concatenate(outs, axis=0).T.astype(o_ref.dtype)


def _mla_attn_call(qt, k, vt):
    B, S, _ = k.shape
    tq, tk = QUERY_TILE, MLA_KEY_BLOCK
    head_keys = lambda hh: pl.BlockSpec((None, S, LANES), lambda b, h, i: (b, 0, 2 * h + hh))
    return pl.pallas_call(
        _mla_attn_kernel,
        grid=(B, MLA_HEADS // 2, S // tq),
        in_specs=[pl.BlockSpec((None, 2 * LANES, tq), lambda b, h, i: (b, h, i)),
                  pl.BlockSpec((None, 2 * LANES, tq), lambda b, h, i: (b, h, jnp.minimum(i + 1, S // tq - 1))),
                  head_keys(0), head_keys(1),
                  pl.BlockSpec((None, S // tk, LANES, tk), lambda b, h, i: (b, 0, h, 0))],
        out_specs=pl.BlockSpec((None, tq, LANES), lambda b, h, i: (b, i, h)),
        out_shape=jax.ShapeDtypeStruct((B, S, MLA_HEADS * MLA_V_DIM), BF16),
        scratch_shapes=_attn_scratch(tq, tk, MLA_V_DIM),
        compiler_params=_params(3),
        name="mla_attn",
    )(qt, qt, k, k, vt)


def _diff_attn_kernel(lq1_ref, lk1_ref, lq2_ref, lk2_ref, gsub_ref, qt_ref, qtn_ref, k_ref, vt_ref, o_ref, *scratch,
                      lambda_init):
    i = pl.program_id(2)
    tk = DIFF_KEY_BLOCK
    ring = QUERY_TILE // tk
    bufs, (acc1_ref, acc2_ref, max_ref) = _split_scratch(scratch, 2, ring)
    feat = lax.broadcasted_iota(jnp.int32, qt_ref.shape, 0)
    k_blk = lambda j: k_ref[_key_rows(j, tk), :]
    vt_blk = lambda j: vt_ref[j]

    def comp(ref, c):
        qt = ref[...]
        return jnp.where((feat >= DIFF_HEAD_DIM) == bool(c), qt, jnp.zeros_like(qt))

    _flash_streams([_Stream(comp(qt_ref, c), comp(qtn_ref, c), max_ref.at[c * ring:(c + 1) * ring], k_blk, vt_blk,
                            bufs[c], acc) for c, acc in enumerate((acc1_ref, acc2_ref))], i, tk)

    lam = (jnp.exp(jnp.sum(lq1_ref[...] * lk1_ref[...], axis=-1, keepdims=True))
           - jnp.exp(jnp.sum(lq2_ref[...] * lk2_ref[...], axis=-1, keepdims=True)) + lambda_init)
    d = _normalised(acc1_ref, DIFF_V_DIM) - lam * _normalised(acc2_ref, DIFF_V_DIM)
    dn = d * lax.rsqrt(jnp.mean(d * d, axis=0, keepdims=True) + NORM_EPS)
    o_ref[...] = (dn.T * gsub_ref[...] * (1.0 - lambda_init)).astype(o_ref.dtype)


def _diff_attn_call(lq1, lk1, lq2, lk2, gsub, qt, k, vt, *, lambda_init):
    B, S, _ = k.shape
    tq, tk = QUERY_TILE, DIFF_KEY_BLOCK
    tile = pl.BlockSpec((None, tq, LANES), lambda b, h, i: (b, i, h))
    return pl.pallas_call(
        functools.partial(_diff_attn_kernel, lambda_init=lambda_init),
        grid=(B, DIFF_HEADS, S // tq),
        in_specs=[_const_spec(a.shape) for a in (lq1, lk1, lq2, lk2, gsub)] + [
            pl.BlockSpec((None, LANES, tq), lambda b, h, i: (b, h, i)),
            pl.BlockSpec((None, LANES, tq), lambda b, h, i: (b, h, jnp.minimum(i + 1, S // tq - 1))),
            pl.BlockSpec((None, S, LANES), lambda b, h, i: (b, 0, h)),
            pl.BlockSpec((None, S // tk, LANES, tk), lambda b, h, i: (b, 0, h, 0))],
        out_specs=tile,
        out_shape=jax.ShapeDtypeStruct((B, S, DIFF_HEADS * DIFF_V_DIM), BF16),
        scratch_shapes=_attn_scratch(tq, tk, DIFF_V_DIM),
        compiler_params=_params(3),
        name="diff_attn",
    )(lq1, lk1, lq2, lk2, gsub, qt, qt, k, vt)


def _post_kernel(x_ref, om_ref, od_ref, gate_ref, wbm_ref, wbd_ref, wo_ref, gffn_ref, wg_ref, wu_ref, wd_ref,
                 gfin_ref, out_ref):
    d_model = x_ref.shape[1]
    gates = gate_ref[...].astype(F32)
    merged = (gates[:, :d_model] * _dot(om_ref[...], wbm_ref[...])
              + gates[:, d_model:] * _dot(od_ref[...], wbd_ref[...]))
    x = x_ref[...] + _dot(merged.astype(BF16), wo_ref[...])
    hn = _rms(x, gffn_ref[...]).astype(BF16)
    n_hidden = wg_ref.shape[1]
    for c0 in range(0, n_hidden, FFN_CHUNK):
        cols = slice(c0, min(c0 + FFN_CHUNK, n_hidden))
        a = _dot(hn, wg_ref[:, cols])
        hid = (a * (1.0 / (1.0 + jnp.exp(-a)))) * _dot(hn, wu_ref[:, cols])
        x = x + _dot(hid.astype(BF16), wd_ref[cols, :])
    out_ref[...] = _rms(x, gfin_ref[...])


def _post_call(x2, om, od, gates, wbm, wbd, wo, gffn, wg, wu, wd, gfin):
    T, D = x2.shape
    tm = POST_TILE
    row = lambda a: pl.BlockSpec((tm, a.shape[1]), lambda i: (i, 0))
    consts = (wbm, wbd, wo, gffn, wg, wu, wd, gfin)
    return pl.pallas_call(
        _post_kernel,
        grid=(T // tm,),
        in_specs=[row(a) for a in (x2, om, od, gates)] + [_const_spec(a.shape) for a in consts],
        out_specs=pl.BlockSpec((tm, D), lambda i: (i, 0)),
        out_shape=jax.ShapeDtypeStruct((T, D), F32),
        compiler_params=_params(1),
        name="post",
    )(x2, om, od, gates, *consts)


def _inv_freq_column(rot_dim):
    half = rot_dim // 2
    return jnp.exp(-math.log(ROPE_THETA) * jnp.arange(half, dtype=F32) * (2.0 / rot_dim)).reshape(half, 1)


def _pad_last(a, width):
    return jnp.pad(a, [(0, 0)] * (a.ndim - 1) + [(0, width - a.shape[-1])])


def kernel(x, positions, norm_mix_g, w_in, b_gate, mla_q_norm_g, mla_w_uq, mla_kv_norm_g, mla_w_ukv, diff_lambda_q1, diff_lambda_k1, diff_lambda_q2, diff_lambda_k2, diff_subln_g, w_branch_mla, w_branch_diff, w_out, norm_ffn_g, w_ffn_gate, w_ffn_up, w_ffn_down, norm_final_g):
    B, S, D = x.shape
    T = B * S
    assert w_in.shape[0] == 1, "single-layer block"
    assert S % QUERY_TILE == 0 and S % PROJ_TILE == 0
    layer = 0
    lambda_init = 0.8 - 0.6 * math.exp(-0.3 * layer)
    row = lambda a: a.reshape(1, -1)

    w = w_in[layer]
    o_kv = MLA_Q_RANK + MLA_KV_RANK
    o_kr = o_kv + MLA_ROPE_DIM
    o_dv = o_kr + 2 * DIFF_HEADS * LANES
    o_gate = o_dv + DIFF_HEADS * DIFF_V_DIM
    o_dk = o_kr + DIFF_HEADS * LANES
    w1 = jnp.concatenate([w[:, :o_kv], w[:, o_gate:]], axis=1).astype(BF16)
    wkpet = jnp.pad(w[:, o_kv:o_kr].T, ((MLA_NOPE_DIM, LANES - MLA_QK_DIM), (0, 0))).astype(BF16)
    wdqt, wdkt, wdvt = (w[:, a:b].T.astype(BF16) for a, b in ((o_kr, o_dk), (o_dk, o_dv), (o_dv, o_gate)))
    wuqt = _pad_last(mla_w_uq[layer].reshape(MLA_Q_RANK, MLA_HEADS, MLA_QK_DIM), LANES)
    wuqt = wuqt.reshape(MLA_Q_RANK, MLA_HEADS * LANES).T.astype(BF16)
    wukv = mla_w_ukv[layer].reshape(MLA_KV_RANK, MLA_HEADS, MLA_NOPE_DIM + MLA_V_DIM)
    wuk = _pad_last(wukv[..., :MLA_NOPE_DIM], LANES).reshape(MLA_KV_RANK, MLA_HEADS * LANES).astype(BF16)
    wuvt = wukv[..., MLA_NOPE_DIM:].reshape(MLA_KV_RANK, MLA_HEADS * MLA_V_DIM).T.astype(BF16)

    qt, k, vt, dqt, dk, dvt, gates = _proj_call(
        x, positions.astype(F32).reshape(B, 1, S),
        (_inv_freq_column(MLA_ROPE_DIM), _inv_freq_column(DIFF_ROT_DIM), row(norm_mix_g[layer]), w1,
         row(mla_q_norm_g[layer]), wuqt, row(mla_kv_norm_g[layer]), wuk, wuvt, wkpet, wdqt, wdkt, wdvt,
         row(b_gate[layer])))

    o_mla = _mla_attn_call(qt, k, vt)
    o_diff = _diff_attn_call(row(diff_lambda_q1[layer]), row(diff_lambda_k1[layer]), row(diff_lambda_q2[layer]),
                             row(diff_lambda_k2[layer]), row(diff_subln_g[layer]), dqt, dk, dvt,
                             lambda_init=lambda_init)

    out = _post_call(x.reshape(T, D), o_mla.reshape(T, -1), o_diff.reshape(T, -1), gates.reshape(T, -1),
                     w_branch_mla[layer].astype(BF16), w_branch_diff[layer].astype(BF16), w_out[layer].astype(BF16),
                     row(norm_ffn_g[layer]), w_ffn_gate[layer].astype(BF16), w_ffn_up[layer].astype(BF16),
                     w_ffn_down[layer].astype(BF16), row(norm_final_g))
    return out.reshape(B, S, D)
```
